```python
import math
import jax, jax.numpy as jnp
from jax import lax
import numpy as np

D_MODEL = 1024
BATCH = 4
SEQ = 8192
DEPTH = 4

N_MIXERS = 2
GDN_HEADS = 8
GDN_HEAD_DIM = 128
GDN_DIM = GDN_HEADS * GDN_HEAD_DIM
GDN_CONV = 4
GDN_CHUNK = 64
S5_DIM = D_MODEL
S5_GROUP = 16
S5_GROUPS = S5_DIM // S5_GROUP
S5_STATE = 64
S5_CHUNK = 128
XA_HEADS = 4
XA_HEAD_DIM = 128
XA_DIM = XA_HEADS * XA_HEAD_DIM
MEM_LEN = 256
D_FF = 4 * D_MODEL
MIX_DIM = GDN_DIM + XA_DIM
GDN_IN = 4 * GDN_DIM + 2 * GDN_HEADS + XA_DIM
S5_IN = S5_DIM + XA_DIM
DN_ALPHA = (2 * DEPTH) ** 0.25
DN_BETA = (8 * DEPTH) ** -0.25
LN_EPS = 1e-5
RMS_EPS = 1e-6
N_GDN_LAYERS = (DEPTH + N_MIXERS - 1) // N_MIXERS
N_S5_LAYERS = DEPTH // N_MIXERS

kernel_name = "hybrid_gdn_s5_memxattn_deepnorm"


def layer_norm(x, g, b):
    xf = x.astype(jnp.float32)
    mu = jnp.mean(xf, axis=-1, keepdims=True)
    var = jnp.mean(jnp.square(xf - mu), axis=-1, keepdims=True)
    return ((xf - mu) * lax.rsqrt(var + LN_EPS) * g.astype(jnp.float32) + b.astype(jnp.float32)).astype(x.dtype)


def l2_normalize(t):
    return t * lax.rsqrt(jnp.sum(jnp.square(t), axis=-1, keepdims=True) + 1e-6)


def causal_depthwise_conv(u, w):
    k_width = w.shape[0]
    length = u.shape[1]
    up = jnp.pad(u, ((0, 0), (k_width - 1, 0), (0, 0)))
    out = up[:, 0:length] * w[0]
    for k in range(1, k_width):
        out = out + up[:, k:k + length] * w[k]
    return out


def gated_delta_rule_chunked(q, k, v, g, beta):
    bsz, length, heads, dk = q.shape
    dv = v.shape[-1]
    c = GDN_CHUNK
    n = length // c

    def to_chunks(t):
        return t.reshape(bsz, n, c, heads, -1).transpose(0, 3, 1, 2, 4)

    q, k, v = to_chunks(q), to_chunks(k), to_chunks(v)
    g = g.reshape(bsz, n, c, heads).transpose(0, 3, 1, 2)
    beta = beta.reshape(bsz, n, c, heads).transpose(0, 3, 1, 2)
    gc = jnp.cumsum(g, axis=-1)

    causal = jnp.tril(jnp.ones((c, c), dtype=bool))
    strict = jnp.tril(jnp.ones((c, c), dtype=bool), k=-1)
    diff = gc[..., :, None] - gc[..., None, :]
    decay = jnp.where(causal, jnp.exp(jnp.where(causal, diff, 0.0)), 0.0)

    kb = k * beta[..., None]
    vb = v * beta[..., None]
    a_strict = jnp.where(strict, jnp.einsum('bhnid,bhnjd->bhnij', kb, k) * decay, 0.0)
    eye = jnp.eye(c, dtype=q.dtype)
    rhs = jnp.concatenate([vb, kb * jnp.exp(gc)[..., None]], axis=-1)
    sol = lax.linalg.triangular_solve(eye + a_strict, rhs, left_side=True, lower=True,
                                      unit_diagonal=True)
    u_blk, w_blk = sol[..., :dv], sol[..., dv:]

    def mv(t):
        return jnp.moveaxis(t, 2, 0)

    def step(state, inp):
        qi, ki, ui, wi, gci, deci = inp
        v_new = ui - jnp.einsum('bhcd,bhde->bhce', wi, state)
        attn = jnp.einsum('bhid,bhjd->bhij', qi, ki) * deci
        o = (jnp.einsum('bhcd,bhde->bhce', qi * jnp.exp(gci)[..., None], state)
             + jnp.einsum('bhij,bhje->bhie', attn, v_new))
        g_last = gci[..., -1]
        k_dec = ki * jnp.exp(g_last[..., None] - gci)[..., None]
        state = state * jnp.exp(g_last)[..., None, None] + jnp.einsum('bhcd,bhce->bhde', k_dec, v_new)
        return state, o

    s0 = jnp.zeros((bsz, heads, dk, dv), dtype=q.dtype)
    _, out = lax.scan(step, s0, (mv(q), mv(k), mv(u_blk), mv(w_blk), mv(gc), mv(decay)))
    return out.transpose(1, 0, 3, 2, 4).reshape(bsz, length, heads, dv)


def gdn_mixer(x, w_in, conv_w, a_log, dt_bias, norm_g):
    bsz, length, _ = x.shape
    f32 = jnp.float32
    proj = x @ w_in
    qkv, z, b_logit, a_logit, xq = jnp.split(
        proj, [3 * GDN_DIM, 4 * GDN_DIM, 4 * GDN_DIM + GDN_HEADS, 4 * GDN_DIM + 2 * GDN_HEADS], axis=-1)
    qkv = jax.nn.silu(causal_depthwise_conv(qkv.astype(f32), conv_w.astype(f32)))
    q, k, v = [t.reshape(bsz, length, GDN_HEADS, GDN_HEAD_DIM) for t in jnp.split(qkv, 3, axis=-1)]
    q = l2_normalize(q) * (GDN_HEAD_DIM ** -0.5)
    k = l2_normalize(k)
    beta = jax.nn.sigmoid(b_logit.astype(f32))
    g = -jnp.exp(a_log.astype(f32)) * jax.nn.softplus(a_logit.astype(f32) + dt_bias.astype(f32))
    o = gated_delta_rule_chunked(q, k, v, g, beta)
    o = o * lax.rsqrt(jnp.mean(jnp.square(o), axis=-1, keepdims=True) + RMS_EPS) * norm_g.astype(f32)
    o = o * jax.nn.silu(z.astype(f32).reshape(bsz, length, GDN_HEADS, GDN_HEAD_DIM))
    return o.reshape(bsz, length, GDN_DIM).astype(x.dtype), xq


def s5_scan(u, abar_re, abar_im, bbar_re, bbar_im, c_re, c_im):
    bsz, length, _ = u.shape
    n = length // S5_CHUNK
    uc = u.reshape(bsz, n, S5_CHUNK, S5_GROUPS, S5_GROUP).transpose(1, 0, 2, 3, 4)
    a_re = jnp.broadcast_to(abar_re, (S5_CHUNK, 1, S5_GROUPS, S5_STATE))
    a_im = jnp.broadcast_to(abar_im, (S5_CHUNK, 1, S5_GROUPS, S5_STATE))

    def combine(e1, e2):
        a1r, a1i, b1r, b1i = e1
        a2r, a2i, b2r, b2i = e2
        return (a2r * a1r - a2i * a1i, a2r * a1i + a2i * a1r,
                a2r * b1r - a2i * b1i + b2r, a2r * b1i + a2i * b1r + b2i)

    def step(h, u_blk):
        h_re, h_im = h
        bu_re = jnp.einsum('bcgi,gpi->cbgp', u_blk, bbar_re)
        bu_im = jnp.einsum('bcgi,gpi->cbgp', u_blk, bbar_im)
        pr, pi, sr, si = lax.associative_scan(combine, (a_re, a_im, bu_re, bu_im), axis=0)
        st_re = sr + pr * h_re - pi * h_im
        st_im = si + pr * h_im + pi * h_re
        y = jnp.einsum('cbgp,gip->bcgi', st_re, c_re) - jnp.einsum('cbgp,gip->bcgi', st_im, c_im)
        return (st_re[-1], st_im[-1]), y

    h0 = jnp.zeros((bsz, S5_GROUPS, S5_STATE), dtype=u.dtype)
    _, ys = lax.scan(step, (h0, h0), uc)
    return ys.transpose(1, 0, 2, 3, 4).reshape(bsz, length, S5_DIM)


def s5_mixer(x, w_in, a_re, a_im, b_re, b_im, c_re, c_im, log_dt, d_skip, w_glu, b_glu):
    f32 = jnp.float32
    proj = x @ w_in
    u, xq = jnp.split(proj, [S5_DIM], axis=-1)
    u = u.astype(f32)
    a_re, a_im = a_re.astype(f32), a_im.astype(f32)
    b_re, b_im = b_re.astype(f32), b_im.astype(f32)
    dt = jnp.exp(log_dt.astype(f32))[:, None]
    mag = jnp.exp(a_re * dt)
    abar_re = mag * jnp.cos(a_im * dt)
    abar_im = mag * jnp.sin(a_im * dt)
    den = jnp.square(a_re) + jnp.square(a_im)
    n_re, n_im = abar_re - 1.0, abar_im
    f_re = (n_re * a_re + n_im * a_im) / den
    f_im = (n_im * a_re - n_re * a_im) / den
    bbar_re = f_re[..., None] * b_re - f_im[..., None] * b_im
    bbar_im = f_re[..., None] * b_im + f_im[..., None] * b_re
    y = s5_scan(u, abar_re, abar_im, bbar_re, bbar_im, c_re.astype(f32), c_im.astype(f32))
    y = y + d_skip.astype(f32) * u
    zg = jax.nn.gelu(y)
    out = zg * jax.nn.sigmoid(zg @ w_glu.astype(f32) + b_glu.astype(f32))
    return out.astype(x.dtype), xq


def memory_attention(xq, mem, w_kv):
    bsz, length, _ = xq.shape
    q = xq.reshape(bsz, length, XA_HEADS, XA_HEAD_DIM)
    k, v = jnp.split(mem @ w_kv, 2, axis=-1)
    k = k.reshape(bsz, -1, XA_HEADS, XA_HEAD_DIM)
    v = v.reshape(bsz, -1, XA_HEADS, XA_HEAD_DIM)
    s = jnp.einsum('blhd,bmhd->bhlm', q, k).astype(jnp.float32) * (XA_HEAD_DIM ** -0.5)
    p = jax.nn.softmax(s, axis=-1).astype(v.dtype)
    o = jnp.einsum('bhlm,bmhd->blhd', p, v)
    return o.reshape(bsz, length, XA_DIM)


def setup_inputs(seed: int = 0) -> dict:
    key = jax.random.key(seed)
    ks = jax.random.split(key, 32)
    nrm = jax.random.normal
    f32 = jnp.float32
    D = D_MODEL
    inp = {}
    inp["x"] = nrm(ks[0], (BATCH, SEQ, D), f32)
    inp["mem"] = nrm(ks[1], (BATCH, MEM_LEN, D), f32)
    inp["w_kv_mem"] = nrm(ks[2], (DEPTH, D, 2 * XA_DIM), f32) * D ** -0.5
    inp["w_o"] = nrm(ks[3], (DEPTH, MIX_DIM, D), f32) * (MIX_DIM ** -0.5) * DN_BETA
    inp["ln1_g"] = 1.0 + 0.02 * nrm(ks[4], (DEPTH, D), f32)
    inp["ln1_b"] = 0.02 * nrm(ks[5], (DEPTH, D), f32)
    inp["ln2_g"] = 1.0 + 0.02 * nrm(ks[6], (DEPTH, D), f32)
    inp["ln2_b"] = 0.02 * nrm(ks[7], (DEPTH, D), f32)
    inp["mlp_w1"] = nrm(ks[8], (DEPTH, D, D_FF), f32) * D ** -0.5
    inp["mlp_w2"] = nrm(ks[9], (DEPTH, D_FF, D), f32) * (D_FF ** -0.5) * DN_BETA
    inp["gdn_w_in"] = nrm(ks[10], (N_GDN_LAYERS, D, GDN_IN), f32) * D ** -0.5
    inp["gdn_conv_w"] = nrm(ks[11], (N_GDN_LAYERS, GDN_CONV, 3 * GDN_DIM), f32) * GDN_CONV ** -0.5
    inp["gdn_a_log"] = jnp.log(jax.random.uniform(ks[12], (N_GDN_LAYERS, GDN_HEADS), f32, 1.0, 16.0))
    dt0 = jnp.exp(jax.random.uniform(ks[13], (N_GDN_LAYERS, GDN_HEADS), f32, math.log(1e-3), math.log(1e-1)))
    inp["gdn_dt_bias"] = dt0 + jnp.log(-jnp.expm1(-dt0))
    inp["gdn_norm_g"] = 1.0 + 0.02 * nrm(ks[14], (N_GDN_LAYERS, GDN_HEAD_DIM), f32)
    inp["s5_w_in"] = nrm(ks[15], (N_S5_LAYERS, D, S5_IN), f32) * D ** -0.5
    sh = (N_S5_LAYERS, S5_GROUPS, S5_STATE)
    inp["s5_a_re"] = -0.5 + 0.01 * nrm(ks[16], sh, f32)
    inp["s5_a_im"] = math.pi * jnp.arange(S5_STATE, dtype=f32) + 0.01 * nrm(ks[17], sh, f32)
    inp["s5_b_re"] = nrm(ks[18], sh + (S5_GROUP,), f32) * (2 * S5_GROUP) ** -0.5
    inp["s5_b_im"] = nrm(ks[19], sh + (S5_GROUP,), f32) * (2 * S5_GROUP) ** -0.5
    shc = (N_S5_LAYERS, S5_GROUPS, S5_GROUP, S5_STATE)
    inp["s5_c_re"] = nrm(ks[20], shc, f32) * (2 * S5_STATE) ** -0.5 * 4.0
    inp["s5_c_im"] = nrm(ks[21], shc, f32) * (2 * S5_STATE) ** -0.5 * 4.0
    inp["s5_log_dt"] = jax.random.uniform(ks[22], (N_S5_LAYERS, S5_GROUPS), f32, math.log(1e-3), math.log(1e-1))
    inp["s5_d"] = nrm(ks[23], (N_S5_LAYERS, S5_DIM), f32)
    inp["s5_w_glu"] = nrm(ks[24], (N_S5_LAYERS, S5_DIM, S5_DIM), f32) * S5_DIM ** -0.5
    inp["s5_b_glu"] = 0.02 * nrm(ks[25], (N_S5_LAYERS, S5_DIM), f32)
    return inp


def reference(x, mem, w_kv_mem, w_o, ln1_g, ln1_b, ln2_g, ln2_b, mlp_w1, mlp_w2,
              gdn_w_in, gdn_conv_w, gdn_a_log, gdn_dt_bias, gdn_norm_g,
              s5_w_in, s5_a_re, s5_a_im, s5_b_re, s5_b_im, s5_c_re, s5_c_im,
              s5_log_dt, s5_d, s5_w_glu, s5_b_glu):
    for i in range(DEPTH):
        j = i // N_MIXERS
        if i % N_MIXERS == 0:
            mix, xq = gdn_mixer(x, gdn_w_in[j], gdn_conv_w[j], gdn_a_log[j], gdn_dt_bias[j], gdn_norm_g[j])
        else:
            mix, xq = s5_mixer(x, s5_w_in[j], s5_a_re[j], s5_a_im[j], s5_b_re[j], s5_b_im[j],
                               s5_c_re[j], s5_c_im[j], s5_log_dt[j], s5_d[j], s5_w_glu[j], s5_b_glu[j])
        cross = memory_attention(xq, mem, w_kv_mem[i])
        h = jnp.concatenate([mix, cross], axis=-1) @ w_o[i]
        x = layer_norm(DN_ALPHA * x + h, ln1_g[i], ln1_b[i])
        f = jnp.square(jax.nn.relu(x @ mlp_w1[i])) @ mlp_w2[i]
        x = layer_norm(DN_ALPHA * x + f, ln2_g[i], ln2_b[i])
    return x
```

```python
import functools

import jax
import jax.numpy as jnp
from jax import lax
from jax.experimental import pallas as pl
from jax.experimental.pallas import tpu as pltpu

F32 = jnp.float32
_MXU_DTYPE = jnp.bfloat16

D_MODEL = 1024
DEPTH = 4
GDN_HEADS = 8
HEAD_DIM = 128
GDN_DIM = GDN_HEADS * HEAD_DIM
GDN_CONV = 4
GDN_CHUNK = 64
S5_DIM = D_MODEL
S5_GROUP = 16
S5_GROUPS = S5_DIM // S5_GROUP
S5_STATE = 64
XA_HEADS = 4
XA_DIM = XA_HEADS * HEAD_DIM
D_FF = 4 * D_MODEL
DN_ALPHA = (2 * DEPTH) ** 0.25
LN_EPS = 1e-5
RMS_EPS = 1e-6

LANES = 128
SUBLANES = 8
_VMEM_LIMIT = 56 * 1024 * 1024

_GDN_TT = 512
_GDN_G = 2 * GDN_CHUNK
_S5_TT = 128
_S5_PITCH = _S5_TT + SUBLANES
_S5_CB = 8
_MM_TM = 1024
_MIX_TM = 512
_MLP_TM = 1024
_MLP_TF = 1024


def _mxu(a):
    return a.astype(_MXU_DTYPE)


def _dot(a, b):
    return jnp.dot(_mxu(a), _mxu(b), preferred_element_type=F32)


def _dot_nt(a, b):
    return lax.dot_general(_mxu(a), _mxu(b), (((1,), (1,)), ((), ())), preferred_element_type=F32)


def _dot_tn(a, b):
    return lax.dot_general(_mxu(a), _mxu(b), (((0,), (0,)), ((), ())), preferred_element_type=F32)


def _dot_exact_lhs(a01, b):
    hi = b.astype(_MXU_DTYPE)
    r1 = b - hi.astype(F32)
    mid = r1.astype(_MXU_DTYPE)
    lo = (r1 - mid.astype(F32)).astype(_MXU_DTYPE)
    a = a01.astype(_MXU_DTYPE)
    acc = jnp.dot(a, lo, preferred_element_type=F32)
    acc = acc + jnp.dot(a, mid, preferred_element_type=F32)
    return acc + jnp.dot(a, hi, preferred_element_type=F32)


def _layer_norm(y, g, b):
    mu = jnp.mean(y, axis=-1, keepdims=True)
    yc = y - mu
    var = jnp.mean(jnp.square(yc), axis=-1, keepdims=True)
    return yc * lax.rsqrt(var + LN_EPS) * g + b


def _softplus(x):
    return jnp.maximum(x, 0.0) + jnp.log1p(jnp.exp(-jnp.abs(x)))


def _matmul_kernel(x_ref, w_ref, o_ref):
    o_ref[...] = jnp.dot(_mxu(x_ref[...]), w_ref[...], preferred_element_type=F32)


def _matmul(x, w, *, tm, tn):
    m, k = x.shape
    n = w.shape[1]
    return pl.pallas_call(
        _matmul_kernel,
        out_shape=jax.ShapeDtypeStruct((m, n), F32),
        grid=(m // tm, n // tn),
        in_specs=[pl.BlockSpec((tm, k), lambda i, j: (i, 0)),
                  pl.BlockSpec((k, tn), lambda i, j: (0, j))],
        out_specs=pl.BlockSpec((tm, tn), lambda i, j: (i, j)),
        compiler_params=pltpu.CompilerParams(
            dimension_semantics=("parallel", "parallel"), vmem_limit_bytes=_VMEM_LIMIT),
    )(x, w)


def _gdn_kernel(q_ref, k_ref, v_ref, z_ref, ba_ref, cwq_ref, cwk_ref, cwv_ref, alog_ref, dtb_ref,
                ng_ref, o_ref, ext_ref, qs_ref, ks_ref, vs_ref, beta_ref, g_ref, s_ref):
    head = pl.program_id(1)
    tt = q_ref.shape[0]
    g_rows = _GDN_G
    c_rows = GDN_CHUNK

    @pl.when(pl.program_id(2) == 0)
    def _():
        ext_ref[:, 0:SUBLANES, :] = jnp.zeros((3, SUBLANES, LANES), F32)
        s_ref[...] = jnp.zeros_like(s_ref)

    conv = []
    for idx, (src, cw) in enumerate(((q_ref, cwq_ref), (k_ref, cwk_ref), (v_ref, cwv_ref))):
        ext_ref[idx, SUBLANES:SUBLANES + tt, :] = src[...]
        base = SUBLANES - (GDN_CONV - 1)
        acc = ext_ref[idx, base:base + tt, :] * cw[0:1, :]
        for tap in range(1, GDN_CONV):
            acc = acc + ext_ref[idx, base + tap:base + tap + tt, :] * cw[tap:tap + 1, :]
        ext_ref[idx, 0:SUBLANES, :] = src[tt - SUBLANES:tt, :]
        conv.append(acc * jax.nn.sigmoid(acc))
    qc, kc, vc = conv
    qs_ref[...] = qc * lax.rsqrt(jnp.sum(jnp.square(qc), axis=-1, keepdims=True) + 1e-6) * (HEAD_DIM ** -0.5)
    ks_ref[...] = kc * lax.rsqrt(jnp.sum(jnp.square(kc), axis=-1, keepdims=True) + 1e-6)
    vs_ref[...] = vc

    ba = ba_ref[...]
    lane = lax.broadcasted_iota(jnp.int32, ba.shape, 1)
    beta_all = jax.nn.sigmoid(ba)
    g_all = -jnp.exp(alog_ref[...]) * _softplus(ba + dtb_ref[...])
    beta = jnp.sum(jnp.where(lane == head, beta_all, 0.0), axis=-1, keepdims=True)
    gdec = jnp.sum(jnp.where(lane == head + GDN_HEADS, g_all, 0.0), axis=-1, keepdims=True)
    beta_ref[...] = jnp.broadcast_to(beta, ba.shape)
    g_ref[...] = jnp.broadcast_to(gdec, ba.shape)

    row = lax.broadcasted_iota(jnp.int32, (g_rows, g_rows), 0)
    col = lax.broadcasted_iota(jnp.int32, (g_rows, g_rows), 1)
    same = (row // c_rows) == (col // c_rows)
    causal = same & (row >= col)
    strict = same & (row > col)
    blk16 = (row // 16) == (col // 16)
    blk32 = (row // 32) == (col // 32)
    eye = (row == col).astype(F32)
    ltri = causal.astype(F32)

    def group(gi, carry):
        r0 = pl.multiple_of(gi * g_rows, g_rows)
        q = qs_ref[pl.ds(r0, g_rows), :]
        k = ks_ref[pl.ds(r0, g_rows), :]
        v = vs_ref[pl.ds(r0, g_rows), :]
        bet = beta_ref[pl.ds(r0, g_rows), :]
        gb = g_ref[pl.ds(r0, g_rows), :]
        gcb = _dot_exact_lhs(ltri, gb)
        diff = gcb - gcb.T
        decay = jnp.where(causal, jnp.exp(jnp.where(causal, diff, 0.0)), 0.0)
        kb = k * bet
        vb = v * bet
        a_mat = jnp.where(strict, _dot_nt(kb, k) * decay, 0.0)
        attn = _dot_nt(q, k) * decay
        d1 = jnp.where(blk16, a_mat, 0.0)
        d2 = _dot(d1, d1)
        d4 = _dot(d2, d2)
        d8 = _dot(d4, d4)
        t_inv = eye - d1
        t_inv = t_inv + _dot(t_inv, d2)
        t_inv = t_inv + _dot(t_inv, d4)
        t_inv = t_inv + _dot(t_inv, d8)
        e1 = jnp.where(blk32 & jnp.logical_not(blk16), a_mat, 0.0)
        t_inv = t_inv - _dot(_dot(t_inv, e1), t_inv)
        e2 = jnp.where(blk32, 0.0, a_mat)
        t_inv = t_inv - _dot(_dot(t_inv, e2), t_inv)
        egc = jnp.exp(gcb)
        uw = _dot(t_inv, jnp.concatenate([vb, kb * egc], axis=1))
        qg = q * egc

        state = s_ref[...]
        v_new, o_inter = [], []
        for c in range(g_rows // c_rows):
            lo = c * c_rows
            hi = lo + c_rows
            vn = uw[lo:hi, :HEAD_DIM] - _dot(uw[lo:hi, HEAD_DIM:], state)
            o_inter.append(_dot(qg[lo:hi], state))
            g_last = gcb[hi - 1:hi, :]
            k_dec = k[lo:hi] * jnp.exp(g_last - gcb[lo:hi])
            state = state * jnp.exp(g_last) + _dot_tn(k_dec, vn)
            v_new.append(vn)
        s_ref[...] = state
        o = jnp.concatenate(o_inter, axis=0) + _dot(attn, jnp.concatenate(v_new, axis=0))

        o = o * lax.rsqrt(jnp.mean(jnp.square(o), axis=-1, keepdims=True) + RMS_EPS) * ng_ref[...]
        z = z_ref[pl.ds(r0, g_rows), :]
        o_ref[pl.ds(r0, g_rows), :] = o * (z * jax.nn.sigmoid(z))
        return carry

    lax.fori_loop(0, tt // g_rows, group, 0)


def _gdn_mixer(proj, ba, conv_w, alog_row, dtb_row, norm_g, *, bsz, length):
    tt = min(_GDN_TT, length)
    nt = length // tt

    def col(off):
        return pl.BlockSpec((tt, HEAD_DIM), lambda b, h, t: (b * nt + t, off + h))

    def cw(off):
        return pl.BlockSpec((GDN_CONV, HEAD_DIM), lambda b, h, t: (0, off + h))

    row_spec = pl.BlockSpec((1, LANES), lambda b, h, t: (0, 0))
    return pl.pallas_call(
        _gdn_kernel,
        out_shape=jax.ShapeDtypeStruct((bsz * length, GDN_DIM), F32),
        grid=(bsz, GDN_HEADS, nt),
        in_specs=[col(0), col(GDN_HEADS), col(2 * GDN_HEADS), col(3 * GDN_HEADS),
                  pl.BlockSpec((tt, LANES), lambda b, h, t: (b * nt + t, 0)),
                  cw(0), cw(GDN_HEADS), cw(2 * GDN_HEADS), row_spec, row_spec, row_spec],
        out_specs=pl.BlockSpec((tt, HEAD_DIM), lambda b, h, t: (b * nt + t, h)),
        scratch_shapes=[pltpu.VMEM((3, tt + SUBLANES, LANES), F32),
                        pltpu.VMEM((tt, LANES), F32), pltpu.VMEM((tt, LANES), F32),
                        pltpu.VMEM((tt, LANES), F32), pltpu.VMEM((tt, LANES), F32),
                        pltpu.VMEM((tt, LANES), F32), pltpu.VMEM((HEAD_DIM, HEAD_DIM), F32)],
        compiler_params=pltpu.CompilerParams(
            dimension_semantics=("parallel", "parallel", "arbitrary"), vmem_limit_bytes=_VMEM_LIMIT),
    )(proj, proj, proj, proj, ba, conv_w, conv_w, conv_w, alog_row, dtb_row, norm_g)


def _s5_discretize_kernel(are_ref, aim_ref, ldt_ref, bre_ref, bim_ref,
                          abre_ref, abim_ref, bbre_ref, bbim_ref):
    a_re = are_ref[...]
    a_im = aim_ref[...]
    dt = jnp.exp(ldt_ref[...])
    mag = jnp.exp(a_re * dt)
    abar_re = mag * jnp.cos(a_im * dt)
    abar_im = mag * jnp.sin(a_im * dt)
    den = jnp.square(a_re) + jnp.square(a_im)
    n_re = abar_re - 1.0
    n_im = abar_im
    f_re = (n_re * a_re + n_im * a_im) / den
    f_im = (n_im * a_re - n_re * a_im) / den
    abre_ref[...] = abar_re
    abim_ref[...] = abar_im
    bbre_ref[...] = f_re * bre_ref[...] - f_im * bim_ref[...]
    bbim_ref[...] = f_re * bim_ref[...] + f_im * bre_ref[...]


def _s5_discretize(a_re, a_im, log_dt, b_re, b_im):
    rows = S5_GROUPS * S5_GROUP
    rep = lambda t: jnp.repeat(t, S5_GROUP, axis=0)
    b2 = lambda t: jnp.transpose(t, (0, 2, 1)).reshape(rows, S5_STATE)
    ldt = jnp.broadcast_to(log_dt[:, None], (S5_GROUPS, S5_STATE))
    shp = jax.ShapeDtypeStruct((rows, S5_STATE), F32)
    return pl.pallas_call(_s5_discretize_kernel, out_shape=(shp, shp, shp, shp))(
        rep(a_re), rep(a_im), rep(ldt), b2(b_re), b2(b_im))


def _s5_kernel(u_ref, wb_ref, wc_ref, are_ref, aim_ref, d_ref, wg_ref, bg_ref, o_ref,
               hs_ref, hre_ref, him_ref):
    nb, tt, _ = u_ref.shape
    pitch = _S5_PITCH
    n_lt = S5_DIM // LANES
    half = n_lt // 2

    @pl.when(pl.program_id(0) == 0)
    def _():
        hre_ref[...] = jnp.zeros_like(hre_ref)
        him_ref[...] = jnp.zeros_like(him_ref)

    for c in range(_S5_CB):
        uc = u_ref[:, :, c * LANES:(c + 1) * LANES].reshape(nb * tt, LANES)
        bu = jnp.dot(_mxu(uc), wb_ref[c], preferred_element_type=F32)
        for b in range(nb):
            s = (c % 2) * nb + b
            for lt in range(n_lt):
                hs_ref[(c // 2) * n_lt + lt, s * pitch:s * pitch + tt, :] = (
                    bu[b * tt:(b + 1) * tt, lt * LANES:(lt + 1) * LANES])

    for cp in range(_S5_CB // 2):
        a_re = [are_ref[cp * half + l] for l in range(half)]
        a_im = [aim_ref[cp * half + l] for l in range(half)]

        def step(t, carry):
            out = []
            for l in range(half):
                h_re, h_im = carry[2 * l], carry[2 * l + 1]
                sl_re = cp * n_lt + l
                sl_im = cp * n_lt + half + l
                bu_re = hs_ref[sl_re, pl.ds(t, SUBLANES, stride=pitch), :]
                bu_im = hs_ref[sl_im, pl.ds(t, SUBLANES, stride=pitch), :]
                n_re = a_re[l] * h_re - a_im[l] * h_im + bu_re
                n_im = a_re[l] * h_im + a_im[l] * h_re + bu_im
                hs_ref[sl_re, pl.ds(t, SUBLANES, stride=pitch), :] = n_re
                hs_ref[sl_im, pl.ds(t, SUBLANES, stride=pitch), :] = n_im
                out += [n_re, n_im]
            return tuple(out)

        init = []
        for l in range(half):
            init += [hre_ref[cp * half + l], him_ref[cp * half + l]]
        fin = lax.fori_loop(0, tt, step, tuple(init))
        for l in range(half):
            hre_ref[cp * half + l] = fin[2 * l]
            him_ref[cp * half + l] = fin[2 * l + 1]

    for b in range(nb):
        ys = []
        for c in range(_S5_CB):
            s = (c % 2) * nb + b
            hcat = jnp.concatenate(
                [hs_ref[(c // 2) * n_lt + lt, s * pitch:s * pitch + tt, :] for lt in range(n_lt)], axis=1)
            ys.append(jnp.dot(_mxu(hcat), wc_ref[c], preferred_element_type=F32))
        y = jnp.concatenate(ys, axis=1) + d_ref[...] * u_ref[b]
        zg = jax.nn.gelu(y)
        gate = jnp.dot(_mxu(zg), wg_ref[...], preferred_element_type=F32) + bg_ref[...]
        o_ref[b] = zg * jax.nn.sigmoid(gate)


def _s5_mixer(proj, wb, wc, a_tab_re, a_tab_im, d_skip, w_glu, b_glu):
    bsz, length, _ = proj.shape
    tt = _S5_TT
    n_slab = (_S5_CB // 2) * (S5_DIM // LANES)
    n_pair = n_slab // 2
    full = lambda shape: pl.BlockSpec(shape, lambda t: (0,) * len(shape))
    return pl.pallas_call(
        _s5_kernel,
        out_shape=jax.ShapeDtypeStruct((bsz, length, S5_DIM), F32),
        grid=(length // tt,),
        in_specs=[pl.BlockSpec((bsz, tt, S5_DIM), lambda t: (0, t, 0)),
                  full(wb.shape), full(wc.shape), full(a_tab_re.shape), full(a_tab_im.shape),
                  full(d_skip.shape), full(w_glu.shape), full(b_glu.shape)],
        out_specs=pl.BlockSpec((bsz, tt, S5_DIM), lambda t: (0, t, 0)),
        scratch_shapes=[pltpu.VMEM((n_slab, SUBLANES * _S5_PITCH, LANES), F32),
                        pltpu.VMEM((n_pair, SUBLANES, LANES), F32),
                        pltpu.VMEM((n_pair, SUBLANES, LANES), F32)],
        compiler_params=pltpu.CompilerParams(
            dimension_semantics=("arbitrary",), vmem_limit_bytes=_VMEM_LIMIT),
    )(proj, wb, wc, a_tab_re, a_tab_im, d_skip, w_glu, b_glu)


def _s5_tables(abar_re, abar_im, bbar_re, bbar_im, c_re, c_im, bsz):
    gpb = S5_GROUPS // _S5_CB
    eye = jnp.eye(gpb, dtype=F32)

    def drive(bb):
        t = bb.reshape(_S5_CB, gpb, S5_GROUP, S5_STATE)
        return jnp.einsum('cgip,gh->cgihp', t, eye).reshape(_S5_CB, gpb * S5_GROUP, gpb * S5_STATE)

    def readout(cc):
        t = cc.reshape(_S5_CB, gpb, S5_GROUP, S5_STATE)
        return jnp.einsum('cgip,gh->chpgi', t, eye).reshape(_S5_CB, gpb * S5_STATE, gpb * S5_GROUP)

    wb = jnp.concatenate([drive(bbar_re), drive(bbar_im)], axis=2).astype(_MXU_DTYPE)
    wc = jnp.concatenate([readout(c_re), -readout(c_im)], axis=1).astype(_MXU_DTYPE)

    def a_tab(ab):
        flat = ab[::S5_GROUP].reshape(_S5_CB // 2, 2, 1, (gpb * S5_STATE) // LANES, LANES)
        t = jnp.broadcast_to(flat, (_S5_CB // 2, 2, bsz, (gpb * S5_STATE) // LANES, LANES))
        t = jnp.transpose(t, (0, 3, 1, 2, 4))
        return t.reshape((_S5_CB // 2) * ((gpb * S5_STATE) // LANES), 2 * bsz, LANES)

    return wb, wc, a_tab(abar_re), a_tab(abar_im)


def _mixout_kernel(mix_ref, xq_ref, k_ref, v_ref, x_ref, wo_ref, g_ref, b_ref, o_ref):
    xq = xq_ref[...]
    cross = []
    for hh in range(XA_HEADS):
        sl = slice(hh * HEAD_DIM, (hh + 1) * HEAD_DIM)
        s = _dot_nt(xq[:, sl], k_ref[:, sl]) * (HEAD_DIM ** -0.5)
        e = jnp.exp(s - jnp.max(s, axis=-1, keepdims=True))
        p = e / jnp.sum(e, axis=-1, keepdims=True)
        cross.append(_dot(p, v_ref[:, sl]))
    mix_dim = mix_ref.shape[1]
    h = jnp.dot(_mxu(mix_ref[...]), wo_ref[0:mix_dim, :], preferred_element_type=F32)
    h = h + jnp.dot(_mxu(jnp.concatenate(cross, axis=1)), wo_ref[mix_dim:, :], preferred_element_type=F32)
    o_ref[...] = _layer_norm(DN_ALPHA * x_ref[...] + h, g_ref[...], b_ref[...])


def _mixout(mix, proj, xq_block, kv, x, w_o, ln_g, ln_b, *, bsz, length):
    tm = min(_MIX_TM, length)
    nt = length // tm
    mem_len = kv.shape[0] // bsz
    row = lambda width: pl.BlockSpec((tm, width), lambda b, t: (b * nt + t, 0))
    vec = pl.BlockSpec((1, D_MODEL), lambda b, t: (0, 0))
    return pl.pallas_call(
        _mixout_kernel,
        out_shape=jax.ShapeDtypeStruct(x.shape, F32),
        grid=(bsz, nt),
        in_specs=[row(mix.shape[1]),
                  pl.BlockSpec((tm, XA_DIM), lambda b, t: (b * nt + t, xq_block)),
                  pl.BlockSpec((mem_len, XA_DIM), lambda b, t: (b, 0)),
                  pl.BlockSpec((mem_len, XA_DIM), lambda b, t: (b, 1)),
                  row(D_MODEL),
                  pl.BlockSpec(w_o.shape, lambda b, t: (0, 0)), vec, vec],
        out_specs=row(D_MODEL),
        compiler_params=pltpu.CompilerParams(
            dimension_semantics=("parallel", "parallel"), vmem_limit_bytes=_VMEM_LIMIT),
    )(mix, proj, kv, kv, x, w_o, ln_g, ln_b)


def _mlp_kernel(x_ref, w1_ref, w2_ref, g_ref, b_ref, o_ref, acc_ref):
    j = pl.program_id(1)
    hid = jnp.dot(_mxu(x_ref[...]), w1_ref[...], preferred_element_type=F32)
    hid = jnp.square(jnp.maximum(hid, 0.0))
    part = jnp.dot(_mxu(hid), w2_ref[...], preferred_element_type=F32)

    @pl.when(j == 0)
    def _():
        acc_ref[...] = part

    @pl.when(j > 0)
    def _():
        acc_ref[...] += part

    @pl.when(j == pl.num_programs(1) - 1)
    def _():
        o_ref[...] = _layer_norm(DN_ALPHA * x_ref[...] + acc_ref[...], g_ref[...], b_ref[...])


def _mlp(x, w1, w2, ln_g, ln_b):
    m = x.shape[0]
    tm = min(_MLP_TM, m)
    tf = _MLP_TF
    vec = pl.BlockSpec((1, D_MODEL), lambda i, j: (0, 0))
    return pl.pallas_call(
        _mlp_kernel,
        out_shape=jax.ShapeDtypeStruct(x.shape, F32),
        grid=(m // tm, D_FF // tf),
        in_specs=[pl.BlockSpec((tm, D_MODEL), lambda i, j: (i, 0)),
                  pl.BlockSpec((D_MODEL, tf), lambda i, j: (0, j)),
                  pl.BlockSpec((tf, D_MODEL), lambda i, j: (j, 0)), vec, vec],
        out_specs=pl.BlockSpec((tm, D_MODEL), lambda i, j: (i, 0)),
        scratch_shapes=[pltpu.VMEM((tm, D_MODEL), F32)],
        compiler_params=pltpu.CompilerParams(
            dimension_semantics=("parallel", "arbitrary"), vmem_limit_bytes=_VMEM_LIMIT),
    )(x, w1, w2, ln_g, ln_b)


def _row(v, offset=0):
    width = -(-(offset + v.shape[0]) // LANES) * LANES
    return jnp.zeros((1, width), F32).at[0, offset:offset + v.shape[0]].set(v.astype(F32))


def kernel(x, mem, w_kv_mem, w_o, ln1_g, ln1_b, ln2_g, ln2_b, mlp_w1, mlp_w2, gdn_w_in, gdn_conv_w,
           gdn_a_log, gdn_dt_bias, gdn_norm_g, s5_w_in, s5_a_re, s5_a_im, s5_b_re, s5_b_im, s5_c_re,
           s5_c_im, s5_log_dt, s5_d, s5_w_glu, s5_b_glu):
    bsz, length, _ = x.shape
    tokens = bsz * length
    mm_tm = min(_MM_TM, tokens)
    xf = x.reshape(tokens, D_MODEL)
    memf = mem.reshape(bsz * mem.shape[1], D_MODEL)
    qkvz = 4 * GDN_DIM
    for i in range(DEPTH):
        j = i // 2
        kv = _matmul(memf, w_kv_mem[i].astype(_MXU_DTYPE), tm=memf.shape[0], tn=2 * XA_DIM)
        if i % 2 == 0:
            w_in = gdn_w_in[j]
            w_main = jnp.concatenate([w_in[:, :qkvz], w_in[:, qkvz + 2 * GDN_HEADS:]], axis=1)
            w_ba = jnp.pad(w_in[:, qkvz:qkvz + 2 * GDN_HEADS], ((0, 0), (0, LANES - 2 * GDN_HEADS)))
            proj = _matmul(xf, w_main.astype(_MXU_DTYPE), tm=mm_tm, tn=(qkvz + XA_DIM) // 3)
            ba = _matmul(xf, w_ba.astype(_MXU_DTYPE), tm=mm_tm, tn=LANES)
            mix = _gdn_mixer(proj, ba, gdn_conv_w[j].astype(F32), _row(gdn_a_log[j], GDN_HEADS),
                             _row(gdn_dt_bias[j], GDN_HEADS), _row(gdn_norm_g[j]),
                             bsz=bsz, length=length)
            xq_block = qkvz // XA_DIM
        else:
            proj = _matmul(xf, s5_w_in[j].astype(_MXU_DTYPE), tm=mm_tm, tn=S5_DIM + XA_DIM)
            abar_re, abar_im, bbar_re, bbar_im = _s5_discretize(
                s5_a_re[j].astype(F32), s5_a_im[j].astype(F32), s5_log_dt[j].astype(F32),
                s5_b_re[j].astype(F32), s5_b_im[j].astype(F32))
            wb, wc, a_tab_re, a_tab_im = _s5_tables(
                abar_re, abar_im, bbar_re, bbar_im, s5_c_re[j].astype(F32), s5_c_im[j].astype(F32), bsz)
            mix = _s5_mixer(proj.reshape(bsz, length, S5_DIM + XA_DIM), wb, wc, a_tab_re, a_tab_im,
                            _row(s5_d[j]), s5_w_glu[j].astype(_MXU_DTYPE), _row(s5_b_glu[j]))
            mix = mix.reshape(tokens, S5_DIM)
            xq_block = S5_DIM // XA_DIM
        xf = _mixout(mix, proj, xq_block, kv, xf, w_o[i].astype(_MXU_DTYPE), _row(ln1_g[i]),
                     _row(ln1_b[i]), bsz=bsz, length=length)
        xf = _mlp(xf, mlp_w1[i].astype(_MXU_DTYPE), mlp_w2[i].astype(_MXU_DTYPE), _row(ln2_g[i]),
                  _row(ln2_b[i]))
    return xf.reshape(bsz, length, D_MODEL)
```

```python
import functools

import jax
import jax.numpy as jnp
from jax import lax
from jax.experimental import pallas as pl
from jax.experimental.pallas import tpu as pltpu

F32 = jnp.float32
_MXU_DTYPE = jnp.bfloat16

D_MODEL = 1024
DEPTH = 4
GDN_HEADS = 8
HEAD_DIM = 128
GDN_DIM = GDN_HEADS * HEAD_DIM
GDN_CONV = 4
GDN_CHUNK = 64
S5_DIM = D_MODEL
S5_GROUP = 16
S5_GROUPS = S5_DIM // S5_GROUP
S5_STATE = 64
XA_HEADS = 4
XA_DIM = XA_HEADS * HEAD_DIM
D_FF = 4 * D_MODEL
DN_ALPHA = (2 * DEPTH) ** 0.25
LN_EPS = 1e-5
RMS_EPS = 1e-6

LANES = 128
SUBLANES = 8
_VMEM_LIMIT = 56 * 1024 * 1024

_GDN_TT = 256
_GDN_HP = 8
_GDN_WIDTH = 8
_GDN_G = 2 * GDN_CHUNK
_S5_TT = 128
_S5_PITCH = _S5_TT + SUBLANES
_S5_CB = 8
_MM_TM = 1024
_MIX_TM = 512
_MLP_TM = 1024
_MLP_TF = 1024


def _mxu(a):
    return a.astype(_MXU_DTYPE)


def _dot(a, b):
    return jnp.dot(_mxu(a), _mxu(b), preferred_element_type=F32)


def _dot_nt(a, b):
    return lax.dot_general(_mxu(a), _mxu(b), (((1,), (1,)), ((), ())), preferred_element_type=F32)


def _dot_tn(a, b):
    return lax.dot_general(_mxu(a), _mxu(b), (((0,), (0,)), ((), ())), preferred_element_type=F32)


def _dot_exact_lhs(a01, b):
    hi = b.astype(_MXU_DTYPE)
    r1 = b - hi.astype(F32)
    mid = r1.astype(_MXU_DTYPE)
    lo = (r1 - mid.astype(F32)).astype(_MXU_DTYPE)
    a = a01.astype(_MXU_DTYPE)
    acc = jnp.dot(a, lo, preferred_element_type=F32)
    acc = acc + jnp.dot(a, mid, preferred_element_type=F32)
    return acc + jnp.dot(a, hi, preferred_element_type=F32)


def _layer_norm(y, g, b):
    mu = jnp.mean(y, axis=-1, keepdims=True)
    yc = y - mu
    var = jnp.mean(jnp.square(yc), axis=-1, keepdims=True)
    return yc * lax.rsqrt(var + LN_EPS) * g + b


def _softplus(x):
    return jnp.maximum(x, 0.0) + jnp.log1p(jnp.exp(-jnp.abs(x)))


def _matmul_kernel(x_ref, w_ref, o_ref):
    o_ref[...] = jnp.dot(_mxu(x_ref[...]), w_ref[...], preferred_element_type=F32)


def _matmul(x, w, *, tm, tn):
    m, k = x.shape
    n = w.shape[1]
    return pl.pallas_call(
        _matmul_kernel,
        out_shape=jax.ShapeDtypeStruct((m, n), F32),
        grid=(m // tm, n // tn),
        in_specs=[pl.BlockSpec((tm, k), lambda i, j: (i, 0)),
                  pl.BlockSpec((k, tn), lambda i, j: (0, j))],
        out_specs=pl.BlockSpec((tm, tn), lambda i, j: (i, j)),
        compiler_params=pltpu.CompilerParams(
            dimension_semantics=("parallel", "parallel"), vmem_limit_bytes=_VMEM_LIMIT),
    )(x, w)


def _gdn_kernel(q_ref, k_ref, v_ref, z_ref, ba_ref, cwq_ref, cwk_ref, cwv_ref, alog_ref, dtb_ref,
                ng_ref, o_ref, ext_ref, qs_ref, ks_ref, vs_ref, uw_ref, qg_ref, kd_ref, attn_ref,
                el_ref, vn_ref, oi_ref, s_ref):
    hgroup = pl.program_id(1)
    tt = q_ref.shape[0]
    hp = q_ref.shape[1] // HEAD_DIM
    g_rows = _GDN_G
    c_rows = GDN_CHUNK
    n_groups = tt // g_rows
    n_chunks = tt // c_rows

    @pl.when(pl.program_id(2) == 0)
    def _():
        ext_ref[:, 0:SUBLANES, :] = jnp.zeros((3, SUBLANES, hp * HEAD_DIM), F32)
        s_ref[...] = jnp.zeros_like(s_ref)

    for idx, (src, cw, dst) in enumerate(((q_ref, cwq_ref, qs_ref), (k_ref, cwk_ref, ks_ref),
                                          (v_ref, cwv_ref, vs_ref))):
        ext_ref[idx, SUBLANES:SUBLANES + tt, :] = src[...]
        base = SUBLANES - (GDN_CONV - 1)
        acc = ext_ref[idx, base:base + tt, :] * cw[0:1, :]
        for tap in range(1, GDN_CONV):
            acc = acc + ext_ref[idx, base + tap:base + tap + tt, :] * cw[tap:tap + 1, :]
        ext_ref[idx, 0:SUBLANES, :] = src[tt - SUBLANES:tt, :]
        dst[...] = acc * jax.nn.sigmoid(acc)

    ba = ba_ref[...]
    lane = lax.broadcasted_iota(jnp.int32, ba.shape, 1)
    beta_all = jax.nn.sigmoid(ba)
    g_all = -jnp.exp(alog_ref[...]) * _softplus(ba + dtb_ref[...])

    row = lax.broadcasted_iota(jnp.int32, (g_rows, g_rows), 0)
    col = lax.broadcasted_iota(jnp.int32, (g_rows, g_rows), 1)
    same = (row // c_rows) == (col // c_rows)
    causal = same & (row >= col)
    strict = same & (row > col)
    blk16 = (row // 16) == (col // 16)
    blk32 = (row // 32) == (col // 32)
    eye = (row == col).astype(F32)
    ltri = causal.astype(F32)

    def every(fn, *lists):
        return [fn(*args) for args in zip(*lists)]

    gate_cols = []
    for hl in range(hp):
        head = hgroup * hp + hl
        gate_cols.append((jnp.sum(jnp.where(lane == head, beta_all, 0.0), axis=-1, keepdims=True),
                          jnp.sum(jnp.where(lane == head + GDN_HEADS, g_all, 0.0), axis=-1, keepdims=True)))

    instances = [(hl, gi) for gi in range(n_groups) for hl in range(hp)]
    for b0 in range(0, len(instances), _GDN_WIDTH):
        batch = instances[b0:b0 + _GDN_WIDTH]
        rows = [slice(gi * g_rows, (gi + 1) * g_rows) for _, gi in batch]
        lanes = [slice(hl * HEAD_DIM, (hl + 1) * HEAD_DIM) for hl, _ in batch]

        def l2n(ref, rs, hs):
            t = ref[rs, hs]
            return t * lax.rsqrt(jnp.sum(jnp.square(t), axis=-1, keepdims=True) + 1e-6)

        q = [l2n(qs_ref, rs, hs) * (HEAD_DIM ** -0.5) for rs, hs in zip(rows, lanes)]
        k = [l2n(ks_ref, rs, hs) for rs, hs in zip(rows, lanes)]
        bet = [jnp.broadcast_to(gate_cols[hl][0][rs], (g_rows, HEAD_DIM)) for (hl, _), rs in zip(batch, rows)]
        gb = [jnp.broadcast_to(gate_cols[hl][1][rs], (g_rows, HEAD_DIM)) for (hl, _), rs in zip(batch, rows)]
        gcb = every(lambda t: _dot_exact_lhs(ltri, t), gb)
        decay = every(lambda t: jnp.where(causal, jnp.exp(jnp.where(causal, t - t.T, 0.0)), 0.0), gcb)
        kb = every(lambda a, b: a * b, k, bet)
        kk = every(_dot_nt, kb, k)
        qk = every(_dot_nt, q, k)
        a_mat = every(lambda m, d: jnp.where(strict, m * d, 0.0), kk, decay)
        for (hl, _), rs, m, d in zip(batch, rows, qk, decay):
            attn_ref[hl, rs, :] = m * d
        d1 = every(lambda a: jnp.where(blk16, a, 0.0), a_mat)
        d2 = every(_dot, d1, d1)
        d4 = every(_dot, d2, d2)
        d8 = every(_dot, d4, d4)
        t_inv = every(lambda d: eye - d, d1)
        for dn in (d2, d4, d8):
            t_inv = every(lambda t, p: t + p, t_inv, every(_dot, t_inv, dn))
        for off_diag in (lambda a: jnp.where(blk32 & jnp.logical_not(blk16), a, 0.0),
                         lambda a: jnp.where(blk32, 0.0, a)):
            te = every(_dot, t_inv, every(off_diag, a_mat))
            t_inv = every(lambda t, p: t - p, t_inv, every(_dot, te, t_inv))
        egc = every(jnp.exp, gcb)
        rhs = [jnp.concatenate([vs_ref[rs, hs] * b, kbi * e], axis=1)
               for rs, hs, b, kbi, e in zip(rows, lanes, bet, kb, egc)]
        uw = every(_dot, t_inv, rhs)
        for (hl, gi), rs, uwi, qi, ki, e, g in zip(batch, rows, uw, q, k, egc, gcb):
            uw_ref[hl, rs, :] = uwi
            qg_ref[hl, rs, :] = qi * e
            for c in range(g_rows // c_rows):
                lo = c * c_rows
                hi = lo + c_rows
                g_last = g[hi - 1:hi, :]
                kd_ref[hl, gi * g_rows + lo:gi * g_rows + hi, :] = ki[lo:hi] * jnp.exp(g_last - g[lo:hi])
                ci = gi * (g_rows // c_rows) + c
                el_ref[hl, ci * SUBLANES:(ci + 1) * SUBLANES, :] = jnp.broadcast_to(
                    jnp.exp(g_last), (SUBLANES, HEAD_DIM))

    states = [s_ref[hl] for hl in range(hp)]
    for ci in range(n_chunks):
        cs = slice(ci * c_rows, (ci + 1) * c_rows)
        lhs = [jnp.concatenate([uw_ref[hl, cs, HEAD_DIM:], qg_ref[hl, cs, :]], axis=0) for hl in range(hp)]
        res = every(_dot, lhs, states)
        vn = [uw_ref[hl, cs, :HEAD_DIM] - res[hl][:c_rows] for hl in range(hp)]
        upd = [_dot_tn(kd_ref[hl, cs, :], vn[hl]) for hl in range(hp)]
        for hl in range(hp):
            vn_ref[hl, cs, :] = vn[hl]
            oi_ref[hl, cs, :] = res[hl][c_rows:]
            states[hl] = states[hl] * el_ref[hl, ci * SUBLANES:ci * SUBLANES + 1, :] + upd[hl]
    for hl in range(hp):
        s_ref[hl] = states[hl]

    for hl, gi in instances:
        hs = slice(hl * HEAD_DIM, (hl + 1) * HEAD_DIM)
        rs = slice(gi * g_rows, (gi + 1) * g_rows)
        o = oi_ref[hl, rs, :] + _dot(attn_ref[hl, rs, :], vn_ref[hl, rs, :])
        o = o * lax.rsqrt(jnp.mean(jnp.square(o), axis=-1, keepdims=True) + RMS_EPS) * ng_ref[...]
        z = z_ref[rs, hs]
        o_ref[rs, hs] = o * (z * jax.nn.sigmoid(z))


def _gdn_mixer(proj, ba, conv_w, alog_row, dtb_row, norm_g, *, bsz, length):
    tt = min(_GDN_TT, length)
    nt = length // tt
    hp = _GDN_HP
    width = hp * HEAD_DIM
    n_hg = GDN_HEADS // hp

    def col(off):
        return pl.BlockSpec((tt, width), lambda b, h, t: (b * nt + t, off * n_hg + h))

    def cw(off):
        return pl.BlockSpec((GDN_CONV, width), lambda b, h, t: (0, off * n_hg + h))

    row_spec = pl.BlockSpec((1, LANES), lambda b, h, t: (0, 0))
    per_head = lambda rows, cols: pltpu.VMEM((hp, rows, cols), F32)
    return pl.pallas_call(
        _gdn_kernel,
        out_shape=jax.ShapeDtypeStruct((bsz * length, GDN_DIM), F32),
        grid=(bsz, n_hg, nt),
        in_specs=[col(0), col(1), col(2), col(3),
                  pl.BlockSpec((tt, LANES), lambda b, h, t: (b * nt + t, 0)),
                  cw(0), cw(1), cw(2), row_spec, row_spec, row_spec],
        out_specs=pl.BlockSpec((tt, width), lambda b, h, t: (b * nt + t, h)),
        scratch_shapes=[pltpu.VMEM((3, tt + SUBLANES, width), F32),
                        pltpu.VMEM((tt, width), F32), pltpu.VMEM((tt, width), F32),
                        pltpu.VMEM((tt, width), F32),
                        per_head(tt, 2 * HEAD_DIM), per_head(tt, HEAD_DIM), per_head(tt, HEAD_DIM),
                        per_head(tt, _GDN_G), per_head((tt // GDN_CHUNK) * SUBLANES, HEAD_DIM),
                        per_head(tt, HEAD_DIM), per_head(tt, HEAD_DIM),
                        per_head(HEAD_DIM, HEAD_DIM)],
        compiler_params=pltpu.CompilerParams(
            dimension_semantics=("parallel", "parallel", "arbitrary"), vmem_limit_bytes=_VMEM_LIMIT),
    )(proj, proj, proj, proj, ba, conv_w, conv_w, conv_w, alog_row, dtb_row, norm_g)


def _s5_discretize_kernel(are_ref, aim_ref, ldt_ref, bre_ref, bim_ref,
                          abre_ref, abim_ref, bbre_ref, bbim_ref):
    a_re = are_ref[...]
    a_im = aim_ref[...]
    dt = jnp.exp(ldt_ref[...])
    mag = jnp.exp(a_re * dt)
    abar_re = mag * jnp.cos(a_im * dt)
    abar_im = mag * jnp.sin(a_im * dt)
    den = jnp.square(a_re) + jnp.square(a_im)
    n_re = abar_re - 1.0
    n_im = abar_im
    f_re = (n_re * a_re + n_im * a_im) / den
    f_im = (n_im * a_re - n_re * a_im) / den
    abre_ref[...] = abar_re
    abim_ref[...] = abar_im
    bbre_ref[...] = f_re * bre_ref[...] - f_im * bim_ref[...]
    bbim_ref[...] = f_re * bim_ref[...] + f_im * bre_ref[...]


def _s5_discretize(a_re, a_im, log_dt, b_re, b_im):
    rows = S5_GROUPS * S5_GROUP
    rep = lambda t: jnp.repeat(t, S5_GROUP, axis=0)
    b2 = lambda t: jnp.transpose(t, (0, 2, 1)).reshape(rows, S5_STATE)
    ldt = jnp.broadcast_to(log_dt[:, None], (S5_GROUPS, S5_STATE))
    shp = jax.ShapeDtypeStruct((rows, S5_STATE), F32)
    return pl.pallas_call(_s5_discretize_kernel, out_shape=(shp, shp, shp, shp))(
        rep(a_re), rep(a_im), rep(ldt), b2(b_re), b2(b_im))


def _s5_kernel(u_ref, wb_ref, wc_ref, are_ref, aim_ref, d_ref, wg_ref, bg_ref, o_ref,
               hs_ref, hre_ref, him_ref):
    nb, tt, _ = u_ref.shape
    pitch = _S5_PITCH
    n_lt = S5_DIM // LANES
    half = n_lt // 2

    @pl.when(pl.program_id(0) == 0)
    def _():
        hre_ref[...] = jnp.zeros_like(hre_ref)
        him_ref[...] = jnp.zeros_like(him_ref)

    for c in range(_S5_CB):
        uc = u_ref[:, :, c * LANES:(c + 1) * LANES].reshape(nb * tt, LANES)
        bu = jnp.dot(_mxu(uc), wb_ref[c], preferred_element_type=F32)
        for b in range(nb):
            s = (c % 2) * nb + b
            for lt in range(n_lt):
                hs_ref[(c // 2) * n_lt + lt, s * pitch:s * pitch + tt, :] = (
                    bu[b * tt:(b + 1) * tt, lt * LANES:(lt + 1) * LANES])

    for cp in range(_S5_CB // 2):
        a_re = [are_ref[cp * half + l] for l in range(half)]
        a_im = [aim_ref[cp * half + l] for l in range(half)]

        def step(t, carry):
            out = []
            for l in range(half):
                h_re, h_im = carry[2 * l], carry[2 * l + 1]
                sl_re = cp * n_lt + l
                sl_im = cp * n_lt + half + l
                bu_re = hs_ref[sl_re, pl.ds(t, SUBLANES, stride=pitch), :]
                bu_im = hs_ref[sl_im, pl.ds(t, SUBLANES, stride=pitch), :]
                n_re = a_re[l] * h_re - a_im[l] * h_im + bu_re
                n_im = a_re[l] * h_im + a_im[l] * h_re + bu_im
                hs_ref[sl_re, pl.ds(t, SUBLANES, stride=pitch), :] = n_re
                hs_ref[sl_im, pl.ds(t, SUBLANES, stride=pitch), :] = n_im
                out += [n_re, n_im]
            return tuple(out)

        init = []
        for l in range(half):
            init += [hre_ref[cp * half + l], him_ref[cp * half + l]]
        fin = lax.fori_loop(0, tt, step, tuple(init))
        for l in range(half):
            hre_ref[cp * half + l] = fin[2 * l]
            him_ref[cp * half + l] = fin[2 * l + 1]

    for b in range(nb):
        ys = []
        for c in range(_S5_CB):
            s = (c % 2) * nb + b
            hcat = jnp.concatenate(
                [hs_ref[(c // 2) * n_lt + lt, s * pitch:s * pitch + tt, :] for lt in range(n_lt)], axis=1)
            ys.append(jnp.dot(_mxu(hcat), wc_ref[c], preferred_element_type=F32))
        y = jnp.concatenate(ys, axis=1) + d_ref[...] * u_ref[b]
        zg = jax.nn.gelu(y)
        gate = jnp.dot(_mxu(zg), wg_ref[...], preferred_element_type=F32) + bg_ref[...]
        o_ref[b] = zg * jax.nn.sigmoid(gate)


def _s5_mixer(proj, wb, wc, a_tab_re, a_tab_im, d_skip, w_glu, b_glu):
    bsz, length, _ = proj.shape
    tt = _S5_TT
    n_slab = (_S5_CB // 2) * (S5_DIM // LANES)
    n_pair = n_slab // 2
    full = lambda shape: pl.BlockSpec(shape, lambda t: (0,) * len(shape))
    return pl.pallas_call(
        _s5_kernel,
        out_shape=jax.ShapeDtypeStruct((bsz, length, S5_DIM), F32),
        grid=(length // tt,),
        in_specs=[pl.BlockSpec((bsz, tt, S5_DIM), lambda t: (0, t, 0)),
                  full(wb.shape), full(wc.shape), full(a_tab_re.shape), full(a_tab_im.shape),
                  full(d_skip.shape), full(w_glu.shape), full(b_glu.shape)],
        out_specs=pl.BlockSpec((bsz, tt, S5_DIM), lambda t: (0, t, 0)),
        scratch_shapes=[pltpu.VMEM((n_slab, SUBLANES * _S5_PITCH, LANES), F32),
                        pltpu.VMEM((n_pair, SUBLANES, LANES), F32),
                        pltpu.VMEM((n_pair, SUBLANES, LANES), F32)],
        compiler_params=pltpu.CompilerParams(
            dimension_semantics=("arbitrary",), vmem_limit_bytes=_VMEM_LIMIT),
    )(proj, wb, wc, a_tab_re, a_tab_im, d_skip, w_glu, b_glu)


def _s5_tables(abar_re, abar_im, bbar_re, bbar_im, c_re, c_im, bsz):
    gpb = S5_GROUPS // _S5_CB
    eye = jnp.eye(gpb, dtype=F32)

    def drive(bb):
        t = bb.reshape(_S5_CB, gpb, S5_GROUP, S5_STATE)
        return jnp.einsum('cgip,gh->cgihp', t, eye).reshape(_S5_CB, gpb * S5_GROUP, gpb * S5_STATE)

    def readout(cc):
        t = cc.reshape(_S5_CB, gpb, S5_GROUP, S5_STATE)
        return jnp.einsum('cgip,gh->chpgi', t, eye).reshape(_S5_CB, gpb * S5_STATE, gpb * S5_GROUP)

    wb = jnp.concatenate([drive(bbar_re), drive(bbar_im)], axis=2).astype(_MXU_DTYPE)
    wc = jnp.concatenate([readout(c_re), -readout(c_im)], axis=1).astype(_MXU_DTYPE)

    def a_tab(ab):
        flat = ab[::S5_GROUP].reshape(_S5_CB // 2, 2, 1, (gpb * S5_STATE) // LANES, LANES)
        t = jnp.broadcast_to(flat, (_S5_CB // 2, 2, bsz, (gpb * S5_STATE) // LANES, LANES))
        t = jnp.transpose(t, (0, 3, 1, 2, 4))
        return t.reshape((_S5_CB // 2) * ((gpb * S5_STATE) // LANES), 2 * bsz, LANES)

    return wb, wc, a_tab(abar_re), a_tab(abar_im)


def _mixout_kernel(mix_ref, xq_ref, k_ref, v_ref, x_ref, wo_ref, g_ref, b_ref, o_ref):
    xq = xq_ref[...]
    cross = []
    for hh in range(XA_HEADS):
        sl = slice(hh * HEAD_DIM, (hh + 1) * HEAD_DIM)
        s = _dot_nt(xq[:, sl], k_ref[:, sl]) * (HEAD_DIM ** -0.5)
        e = jnp.exp(s - jnp.max(s, axis=-1, keepdims=True))
        p = e / jnp.sum(e, axis=-1, keepdims=True)
        cross.append(_dot(p, v_ref[:, sl]))
    mix_dim = mix_ref.shape[1]
    h = jnp.dot(_mxu(mix_ref[...]), wo_ref[0:mix_dim, :], preferred_element_type=F32)
    h = h + jnp.dot(_mxu(jnp.concatenate(cross, axis=1)), wo_ref[mix_dim:, :], preferred_element_type=F32)
    o_ref[...] = _layer_norm(DN_ALPHA * x_ref[...] + h, g_ref[...], b_ref[...])


def _mixout(mix, proj, xq_block, kv, x, w_o, ln_g, ln_b, *, bsz, length):
    tm = min(_MIX_TM, length)
    nt = length // tm
    mem_len = kv.shape[0] // bsz
    row = lambda width: pl.BlockSpec((tm, width), lambda b, t: (b * nt + t, 0))
    vec = pl.BlockSpec((1, D_MODEL), lambda b, t: (0, 0))
    return pl.pallas_call(
        _mixout_kernel,
        out_shape=jax.ShapeDtypeStruct(x.shape, F32),
        grid=(bsz, nt),
        in_specs=[row(mix.shape[1]),
                  pl.BlockSpec((tm, XA_DIM), lambda b, t: (b * nt + t, xq_block)),
                  pl.BlockSpec((mem_len, XA_DIM), lambda b, t: (b, 0)),
                  pl.BlockSpec((mem_len, XA_DIM), lambda b, t: (b, 1)),
                  row(D_MODEL),
                  pl.BlockSpec(w_o.shape, lambda b, t: (0, 0)), vec, vec],
        out_specs=row(D_MODEL),
        compiler_params=pltpu.CompilerParams(
            dimension_semantics=("parallel", "parallel"), vmem_limit_bytes=_VMEM_LIMIT),
    )(mix, proj, kv, kv, x, w_o, ln_g, ln_b)


def _mlp_kernel(x_ref, w1_ref, w2_ref, g_ref, b_ref, o_ref, acc_ref):
    j = pl.program_id(1)
    hid = jnp.dot(_mxu(x_ref[...]), w1_ref[...], preferred_element_type=F32)
    hid = jnp.square(jnp.maximum(hid, 0.0))
    part = jnp.dot(_mxu(hid), w2_ref[...], preferred_element_type=F32)

    @pl.when(j == 0)
    def _():
        acc_ref[...] = part

    @pl.when(j > 0)
    def _():
        acc_ref[...] += part

    @pl.when(j == pl.num_programs(1) - 1)
    def _():
        o_ref[...] = _layer_norm(DN_ALPHA * x_ref[...] + acc_ref[...], g_ref[...], b_ref[...])


def _mlp(x, w1, w2, ln_g, ln_b):
    m = x.shape[0]
    tm = min(_MLP_TM, m)
    tf = _MLP_TF
    vec = pl.BlockSpec((1, D_MODEL), lambda i, j: (0, 0))
    return pl.pallas_call(
        _mlp_kernel,
        out_shape=jax.ShapeDtypeStruct(x.shape, F32),
        grid=(m // tm, D_FF // tf),
        in_specs=[pl.BlockSpec((tm, D_MODEL), lambda i, j: (i, 0)),
                  pl.BlockSpec((D_MODEL, tf), lambda i, j: (0, j)),
                  pl.BlockSpec((tf, D_MODEL), lambda i, j: (j, 0)), vec, vec],
        out_specs=pl.BlockSpec((tm, D_MODEL), lambda i, j: (i, 0)),
        scratch_shapes=[pltpu.VMEM((tm, D_MODEL), F32)],
        compiler_params=pltpu.CompilerParams(
            dimension_semantics=("parallel", "arbitrary"), vmem_limit_bytes=_VMEM_LIMIT),
    )(x, w1, w2, ln_g, ln_b)


def _row(v, offset=0):
    width = -(-(offset + v.shape[0]) // LANES) * LANES
    return jnp.zeros((1, width), F32).at[0, offset:offset + v.shape[0]].set(v.astype(F32))


def kernel(x, mem, w_kv_mem, w_o, ln1_g, ln1_b, ln2_g, ln2_b, mlp_w1, mlp_w2, gdn_w_in, gdn_conv_w,
           gdn_a_log, gdn_dt_bias, gdn_norm_g, s5_w_in, s5_a_re, s5_a_im, s5_b_re, s5_b_im, s5_c_re,
           s5_c_im, s5_log_dt, s5_d, s5_w_glu, s5_b_glu):
    bsz, length, _ = x.shape
    tokens = bsz * length
    mm_tm = min(_MM_TM, tokens)
    xf = x.reshape(tokens, D_MODEL)
    memf = mem.reshape(bsz * mem.shape[1], D_MODEL)
    qkvz = 4 * GDN_DIM
    for i in range(DEPTH):
        j = i // 2
        kv = _matmul(memf, w_kv_mem[i].astype(_MXU_DTYPE), tm=memf.shape[0], tn=2 * XA_DIM)
        if i % 2 == 0:
            w_in = gdn_w_in[j]
            w_main = jnp.concatenate([w_in[:, :qkvz], w_in[:, qkvz + 2 * GDN_HEADS:]], axis=1)
            w_ba = jnp.pad(w_in[:, qkvz:qkvz + 2 * GDN_HEADS], ((0, 0), (0, LANES - 2 * GDN_HEADS)))
            proj = _matmul(xf, w_main.astype(_MXU_DTYPE), tm=mm_tm, tn=(qkvz + XA_DIM) // 3)
            ba = _matmul(xf, w_ba.astype(_MXU_DTYPE), tm=mm_tm, tn=LANES)
            mix = _gdn_mixer(proj, ba, gdn_conv_w[j].astype(F32), _row(gdn_a_log[j], GDN_HEADS),
                             _row(gdn_dt_bias[j], GDN_HEADS), _row(gdn_norm_g[j]),
                             bsz=bsz, length=length)
            xq_block = qkvz // XA_DIM
        else:
            proj = _matmul(xf, s5_w_in[j].astype(_MXU_DTYPE), tm=mm_tm, tn=S5_DIM + XA_DIM)
            abar_re, abar_im, bbar_re, bbar_im = _s5_discretize(
                s5_a_re[j].astype(F32), s5_a_im[j].astype(F32), s5_log_dt[j].astype(F32),
                s5_b_re[j].astype(F32), s5_b_im[j].astype(F32))
            wb, wc, a_tab_re, a_tab_im = _s5_tables(
                abar_re, abar_im, bbar_re, bbar_im, s5_c_re[j].astype(F32), s5_c_im[j].astype(F32), bsz)
            mix = _s5_mixer(proj.reshape(bsz, length, S5_DIM + XA_DIM), wb, wc, a_tab_re, a_tab_im,
                            _row(s5_d[j]), s5_w_glu[j].astype(_MXU_DTYPE), _row(s5_b_glu[j]))
            mix = mix.reshape(tokens, S5_DIM)
            xq_block = S5_DIM // XA_DIM
        xf = _mixout(mix, proj, xq_block, kv, xf, w_o[i].astype(_MXU_DTYPE), _row(ln1_g[i]),
                     _row(ln1_b[i]), bsz=bsz, length=length)
        xf = _mlp(xf, mlp_w1[i].astype(_MXU_DTYPE), mlp_w2[i].astype(_MXU_DTYPE), _row(ln2_g[i]),
                  _row(ln2_b[i]))
    return xf.reshape(bsz, length, D_MODEL)
```

```python
import functools

import jax
import jax.numpy as jnp
from jax import lax
from jax.experimental import pallas as pl
from jax.experimental.pallas import tpu as pltpu

F32 = jnp.float32
_MXU_DTYPE = jnp.bfloat16

D_MODEL = 1024
DEPTH = 4
GDN_HEADS = 8
HEAD_DIM = 128
GDN_DIM = GDN_HEADS * HEAD_DIM
GDN_CONV = 4
GDN_CHUNK = 64
S5_DIM = D_MODEL
S5_GROUP = 16
S5_GROUPS = S5_DIM // S5_GROUP
S5_STATE = 64
XA_HEADS = 4
XA_DIM = XA_HEADS * HEAD_DIM
D_FF = 4 * D_MODEL
DN_ALPHA = (2 * DEPTH) ** 0.25
LN_EPS = 1e-5
RMS_EPS = 1e-6

LANES = 128
SUBLANES = 8
_VMEM_LIMIT = 56 * 1024 * 1024

_GDN_TT = 256
_GDN_WIDTH = 8
_GDN_G = 2 * GDN_CHUNK
_S5_TT = 128
_S5_UNROLL = 4
_S5_PITCH = _S5_TT + SUBLANES
_S5_CB = 8
_MIX_TM = 512
_MLP_TM = 1024
_MLP_TF = 1024


def _mxu(a):
    return a.astype(_MXU_DTYPE)


def _dot(a, b):
    return jnp.dot(_mxu(a), _mxu(b), preferred_element_type=F32)


def _dot_nt(a, b):
    return lax.dot_general(_mxu(a), _mxu(b), (((1,), (1,)), ((), ())), preferred_element_type=F32)


def _dot_tn(a, b):
    return lax.dot_general(_mxu(a), _mxu(b), (((0,), (0,)), ((), ())), preferred_element_type=F32)


def _dot_exact_lhs(a01, b):
    hi = b.astype(_MXU_DTYPE)
    r1 = b - hi.astype(F32)
    mid = r1.astype(_MXU_DTYPE)
    lo = (r1 - mid.astype(F32)).astype(_MXU_DTYPE)
    a = a01.astype(_MXU_DTYPE)
    acc = jnp.dot(a, lo, preferred_element_type=F32)
    acc = acc + jnp.dot(a, mid, preferred_element_type=F32)
    return acc + jnp.dot(a, hi, preferred_element_type=F32)


def _layer_norm(y, g, b):
    mu = jnp.mean(y, axis=-1, keepdims=True)
    yc = y - mu
    var = jnp.mean(jnp.square(yc), axis=-1, keepdims=True)
    return yc * lax.rsqrt(var + LN_EPS) * g + b


def _softplus(x):
    return jnp.maximum(x, 0.0) + jnp.log1p(jnp.exp(-jnp.abs(x)))


def _matmul_kernel(x_ref, w_ref, o_ref):
    o_ref[...] = jnp.dot(_mxu(x_ref[...]), w_ref[...], preferred_element_type=F32)


def _matmul(x, w, *, tm, tn):
    m, k = x.shape
    n = w.shape[1]
    return pl.pallas_call(
        _matmul_kernel,
        out_shape=jax.ShapeDtypeStruct((m, n), F32),
        grid=(m // tm, n // tn),
        in_specs=[pl.BlockSpec((tm, k), lambda i, j: (i, 0)),
                  pl.BlockSpec((k, tn), lambda i, j: (0, j))],
        out_specs=pl.BlockSpec((tm, tn), lambda i, j: (i, j)),
        compiler_params=pltpu.CompilerParams(
            dimension_semantics=("parallel", "parallel"), vmem_limit_bytes=_VMEM_LIMIT),
    )(x, w)


def _gdn_kernel(x_ref, w_ref, wba_ref, cw_ref, alog_ref, dtb_ref, ng_ref, o_ref,
                tail_ref, qs_ref, ks_ref, vs_ref, zs_ref, uw_ref, qg_ref, kd_ref, attn_ref,
                el_ref, vn_ref, oi_ref, op_ref, s_ref):
    tt = x_ref.shape[0]
    hp = GDN_HEADS
    g_rows = _GDN_G
    c_rows = GDN_CHUNK
    n_groups = tt // g_rows
    n_chunks = tt // c_rows
    vpc = c_rows // SUBLANES
    n_wrap = GDN_CONV - 1

    @pl.when(pl.program_id(1) == 0)
    def _():
        tail_ref[...] = jnp.zeros_like(tail_ref)
        s_ref[...] = jnp.zeros_like(s_ref)

    def time_of(pos):
        local = pos % c_rows
        return (pos // c_rows) * c_rows + local // SUBLANES + SUBLANES * (local % SUBLANES)

    prow = lax.broadcasted_iota(jnp.int32, (tt, tt), 0)
    pcol = lax.broadcasted_iota(jnp.int32, (tt, tt), 1)
    perm = (pcol == time_of(prow)).astype(_MXU_DTYPE)
    xp = jnp.dot(perm, _mxu(x_ref[...]), preferred_element_type=F32).astype(_MXU_DTYPE)

    rowi = lax.broadcasted_iota(jnp.int32, (n_chunks * SUBLANES, GDN_DIM), 0)
    for idx, dst in enumerate((qs_ref, ks_ref, vs_ref)):
        sec = slice(idx * GDN_DIM, (idx + 1) * GDN_DIM)
        y = jnp.dot(xp, w_ref[:, sec], preferred_element_type=F32)

        def slab(c, j):
            return y[c * c_rows + j * SUBLANES:c * c_rows + (j + 1) * SUBLANES, :]

        shifted = {}
        for v in range(vpc - n_wrap, vpc):
            wv = jnp.concatenate([slab(c, v) for c in range(n_chunks)], axis=0)
            carry = tail_ref[idx, v - (vpc - n_wrap), SUBLANES - 1:SUBLANES, :]
            shifted[v] = jnp.where(rowi == 0, carry, pltpu.roll(wv, 1, axis=0))
            tail_ref[idx, v - (vpc - n_wrap)] = slab(n_chunks - 1, v)
        taps = [cw_ref[tap:tap + 1, sec] for tap in range(GDN_CONV)]
        for c in range(n_chunks):
            for j in range(vpc):
                def src(d):
                    if j >= d:
                        return slab(c, j - d)
                    return shifted[j - d + vpc][c * SUBLANES:(c + 1) * SUBLANES, :]
                acc = src(3) * taps[0]
                for tap in range(1, GDN_CONV):
                    acc = acc + src(GDN_CONV - 1 - tap) * taps[tap]
                dst[c * c_rows + j * SUBLANES:c * c_rows + (j + 1) * SUBLANES, :] = acc * jax.nn.sigmoid(acc)
    zs_ref[...] = jnp.dot(xp, w_ref[:, 3 * GDN_DIM:], preferred_element_type=F32)

    ba = jnp.dot(xp, wba_ref[...], preferred_element_type=F32)
    lane = lax.broadcasted_iota(jnp.int32, ba.shape, 1)
    beta_all = jax.nn.sigmoid(ba)
    g_all = -jnp.exp(alog_ref[...]) * _softplus(ba + dtb_ref[...])

    trow = time_of(lax.broadcasted_iota(jnp.int32, (g_rows, g_rows), 0))
    tcol = time_of(lax.broadcasted_iota(jnp.int32, (g_rows, g_rows), 1))
    same = (trow // c_rows) == (tcol // c_rows)
    causal = same & (trow >= tcol)
    strict = same & (trow > tcol)
    blk16 = (trow // 16) == (tcol // 16)
    blk32 = (trow // 32) == (tcol // 32)
    eye = (trow == tcol).astype(F32)
    ltri = causal.astype(F32)

    def every(fn, *lists):
        return [fn(*args) for args in zip(*lists)]

    gate_cols = []
    for hl in range(hp):
        gate_cols.append((jnp.sum(jnp.where(lane == hl, beta_all, 0.0), axis=-1, keepdims=True),
                          jnp.sum(jnp.where(lane == hl + GDN_HEADS, g_all, 0.0), axis=-1, keepdims=True)))

    instances = [(hl, gi) for gi in range(n_groups) for hl in range(hp)]
    for b0 in range(0, len(instances), _GDN_WIDTH):
        batch = instances[b0:b0 + _GDN_WIDTH]
        rows = [slice(gi * g_rows, (gi + 1) * g_rows) for _, gi in batch]
        lanes = [slice(hl * HEAD_DIM, (hl + 1) * HEAD_DIM) for hl, _ in batch]

        def l2n(ref, rs, hs):
            t = ref[rs, hs]
            return t * lax.rsqrt(jnp.sum(jnp.square(t), axis=-1, keepdims=True) + 1e-6)

        q = [l2n(qs_ref, rs, hs) * (HEAD_DIM ** -0.5) for rs, hs in zip(rows, lanes)]
        k = [l2n(ks_ref, rs, hs) for rs, hs in zip(rows, lanes)]
        bet = [jnp.broadcast_to(gate_cols[hl][0][rs], (g_rows, HEAD_DIM)) for (hl, _), rs in zip(batch, rows)]
        gb = [jnp.broadcast_to(gate_cols[hl][1][rs], (g_rows, HEAD_DIM)) for (hl, _), rs in zip(batch, rows)]
        gcb = every(lambda t: _dot_exact_lhs(ltri, t), gb)
        decay = every(lambda t: jnp.where(causal, jnp.exp(jnp.where(causal, t - t.T, 0.0)), 0.0), gcb)
        kb = every(lambda a, b: a * b, k, bet)
        kk = every(_dot_nt, kb, k)
        qk = every(_dot_nt, q, k)
        a_mat = every(lambda m, d: jnp.where(strict, m * d, 0.0), kk, decay)
        for (hl, _), rs, m, d in zip(batch, rows, qk, decay):
            attn_ref[hl, rs, :] = m * d
        d1 = every(lambda a: jnp.where(blk16, a, 0.0), a_mat)
        d2 = every(_dot, d1, d1)
        d4 = every(_dot, d2, d2)
        d8 = every(_dot, d4, d4)
        t_inv = every(lambda d: eye - d, d1)
        for dn in (d2, d4, d8):
            t_inv = every(lambda t, p: t + p, t_inv, every(_dot, t_inv, dn))
        for off_diag in (lambda a: jnp.where(blk32 & jnp.logical_not(blk16), a, 0.0),
                         lambda a: jnp.where(blk32, 0.0, a)):
            te = every(_dot, t_inv, every(off_diag, a_mat))
            t_inv = every(lambda t, p: t - p, t_inv, every(_dot, te, t_inv))
        egc = every(jnp.exp, gcb)
        rhs = [jnp.concatenate([vs_ref[rs, hs] * b, kbi * e], axis=1)
               for rs, hs, b, kbi, e in zip(rows, lanes, bet, kb, egc)]
        uw = every(_dot, t_inv, rhs)
        for (hl, gi), rs, uwi, qi, ki, e, g in zip(batch, rows, uw, q, k, egc, gcb):
            uw_ref[hl, rs, :] = uwi
            qg_ref[hl, rs, :] = qi * e
            for c in range(g_rows // c_rows):
                lo = c * c_rows
                hi = lo + c_rows
                g_last = g[hi - 1:hi, :]
                kd_ref[hl, gi * g_rows + lo:gi * g_rows + hi, :] = ki[lo:hi] * jnp.exp(g_last - g[lo:hi])
                ci = gi * (g_rows // c_rows) + c
                el_ref[hl, ci * SUBLANES:(ci + 1) * SUBLANES, :] = jnp.broadcast_to(
                    jnp.exp(g_last), (SUBLANES, HEAD_DIM))

    states = [s_ref[hl] for hl in range(hp)]
    for ci in range(n_chunks):
        cs = slice(ci * c_rows, (ci + 1) * c_rows)
        lhs = [jnp.concatenate([uw_ref[hl, cs, HEAD_DIM:], qg_ref[hl, cs, :]], axis=0) for hl in range(hp)]
        res = every(_dot, lhs, states)
        vn = [uw_ref[hl, cs, :HEAD_DIM] - res[hl][:c_rows] for hl in range(hp)]
        upd = [_dot_tn(kd_ref[hl, cs, :], vn[hl]) for hl in range(hp)]
        for hl in range(hp):
            vn_ref[hl, cs, :] = vn[hl]
            oi_ref[hl, cs, :] = res[hl][c_rows:]
            states[hl] = states[hl] * el_ref[hl, ci * SUBLANES:ci * SUBLANES + 1, :] + upd[hl]
    for hl in range(hp):
        s_ref[hl] = states[hl]

    for hl, gi in instances:
        hs = slice(hl * HEAD_DIM, (hl + 1) * HEAD_DIM)
        rs = slice(gi * g_rows, (gi + 1) * g_rows)
        o = oi_ref[hl, rs, :] + _dot(attn_ref[hl, rs, :], vn_ref[hl, rs, :])
        o = o * lax.rsqrt(jnp.mean(jnp.square(o), axis=-1, keepdims=True) + RMS_EPS) * ng_ref[...]
        z = zs_ref[rs, hs]
        op_ref[rs, hs] = (o * (z * jax.nn.sigmoid(z))).astype(op_ref.dtype)
    o_ref[...] = jnp.dot(perm, op_ref[...], preferred_element_type=F32).astype(o_ref.dtype)


def _gdn_mixer(x, w_qkvz, w_ba, conv_w, alog_row, dtb_row, norm_g, *, bsz, length):
    tt = min(_GDN_TT, length)
    nt = length // tt
    full = lambda a: pl.BlockSpec(a.shape, lambda b, t: (0,) * a.ndim)
    per_head = lambda rows, cols: pltpu.VMEM((GDN_HEADS, rows, cols), F32)
    tile = lambda: pltpu.VMEM((tt, GDN_DIM), F32)
    return pl.pallas_call(
        _gdn_kernel,
        out_shape=jax.ShapeDtypeStruct((bsz * length, GDN_DIM), _MXU_DTYPE),
        grid=(bsz, nt),
        in_specs=[pl.BlockSpec((tt, D_MODEL), lambda b, t: (b * nt + t, 0)),
                  full(w_qkvz), full(w_ba), full(conv_w), full(alog_row), full(dtb_row), full(norm_g)],
        out_specs=pl.BlockSpec((tt, GDN_DIM), lambda b, t: (b * nt + t, 0)),
        scratch_shapes=[pltpu.VMEM((3, GDN_CONV - 1, SUBLANES, GDN_DIM), F32),
                        tile(), tile(), tile(), tile(),
                        per_head(tt, 2 * HEAD_DIM), per_head(tt, HEAD_DIM), per_head(tt, HEAD_DIM),
                        per_head(tt, _GDN_G), per_head((tt // GDN_CHUNK) * SUBLANES, HEAD_DIM),
                        per_head(tt, HEAD_DIM), per_head(tt, HEAD_DIM),
                        pltpu.VMEM((tt, GDN_DIM), _MXU_DTYPE),
                        per_head(HEAD_DIM, HEAD_DIM)],
        compiler_params=pltpu.CompilerParams(
            dimension_semantics=("parallel", "arbitrary"), vmem_limit_bytes=_VMEM_LIMIT),
    )(x, w_qkvz, w_ba, conv_w, alog_row, dtb_row, norm_g)


def _s5_discretize_kernel(are_ref, aim_ref, ldt_ref, bre_ref, bim_ref,
                          abre_ref, abim_ref, bbre_ref, bbim_ref):
    a_re = are_ref[...]
    a_im = aim_ref[...]
    dt = jnp.exp(ldt_ref[...])
    mag = jnp.exp(a_re * dt)
    abar_re = mag * jnp.cos(a_im * dt)
    abar_im = mag * jnp.sin(a_im * dt)
    den = jnp.square(a_re) + jnp.square(a_im)
    n_re = abar_re - 1.0
    n_im = abar_im
    f_re = (n_re * a_re + n_im * a_im) / den
    f_im = (n_im * a_re - n_re * a_im) / den
    abre_ref[...] = abar_re
    abim_ref[...] = abar_im
    bbre_ref[...] = f_re * bre_ref[...] - f_im * bim_ref[...]
    bbim_ref[...] = f_re * bim_ref[...] + f_im * bre_ref[...]


def _s5_discretize(a_re, a_im, log_dt, b_re, b_im):
    rows = S5_GROUPS * S5_GROUP
    rep = lambda t: jnp.repeat(t, S5_GROUP, axis=0)
    b2 = lambda t: jnp.transpose(t, (0, 2, 1)).reshape(rows, S5_STATE)
    ldt = jnp.broadcast_to(log_dt[:, None], (S5_GROUPS, S5_STATE))
    shp = jax.ShapeDtypeStruct((rows, S5_STATE), F32)
    return pl.pallas_call(_s5_discretize_kernel, out_shape=(shp, shp, shp, shp))(
        rep(a_re), rep(a_im), rep(ldt), b2(b_re), b2(b_im))


def _s5_kernel(x_ref, wu_ref, wb_ref, wc_ref, are_ref, aim_ref, d_ref, wg_ref, bg_ref, o_ref,
               us_ref, hs_ref, hre_ref, him_ref):
    nb, tt, _ = x_ref.shape
    pitch = _S5_PITCH
    n_lt = S5_DIM // LANES
    half = n_lt // 2

    @pl.when(pl.program_id(0) == 0)
    def _():
        hre_ref[...] = jnp.zeros_like(hre_ref)
        him_ref[...] = jnp.zeros_like(him_ref)

    us_ref[...] = jnp.dot(_mxu(x_ref[...].reshape(nb * tt, D_MODEL)), wu_ref[...],
                          preferred_element_type=F32)

    for c in range(_S5_CB):
        uc = us_ref[:, c * LANES:(c + 1) * LANES]
        bu = jnp.dot(_mxu(uc), wb_ref[c], preferred_element_type=F32)
        for b in range(nb):
            s = (c % 2) * nb + b
            for lt in range(n_lt):
                hs_ref[(c // 2) * n_lt + lt, s * pitch:s * pitch + tt, :] = (
                    bu[b * tt:(b + 1) * tt, lt * LANES:(lt + 1) * LANES])

    for cp in range(_S5_CB // 2):
        a_re = [are_ref[cp * half + l] for l in range(half)]
        a_im = [aim_ref[cp * half + l] for l in range(half)]

        def step(t, carry):
            out = []
            for l in range(half):
                h_re, h_im = carry[2 * l], carry[2 * l + 1]
                sl_re = cp * n_lt + l
                sl_im = cp * n_lt + half + l
                bu_re = hs_ref[sl_re, pl.ds(t, SUBLANES, stride=pitch), :]
                bu_im = hs_ref[sl_im, pl.ds(t, SUBLANES, stride=pitch), :]
                n_re = a_re[l] * h_re - a_im[l] * h_im + bu_re
                n_im = a_re[l] * h_im + a_im[l] * h_re + bu_im
                hs_ref[sl_re, pl.ds(t, SUBLANES, stride=pitch), :] = n_re
                hs_ref[sl_im, pl.ds(t, SUBLANES, stride=pitch), :] = n_im
                out += [n_re, n_im]
            return tuple(out)

        init = []
        for l in range(half):
            init += [hre_ref[cp * half + l], him_ref[cp * half + l]]
        fin = lax.fori_loop(0, tt, step, tuple(init), unroll=_S5_UNROLL)
        for l in range(half):
            hre_ref[cp * half + l] = fin[2 * l]
            him_ref[cp * half + l] = fin[2 * l + 1]

    for b in range(nb):
        ys = []
        for c in range(_S5_CB):
            s = (c % 2) * nb + b
            hcat = jnp.concatenate(
                [hs_ref[(c // 2) * n_lt + lt, s * pitch:s * pitch + tt, :] for lt in range(n_lt)], axis=1)
            ys.append(jnp.dot(_mxu(hcat), wc_ref[c], preferred_element_type=F32))
        y = jnp.concatenate(ys, axis=1) + d_ref[...] * us_ref[b * tt:(b + 1) * tt, :]
        zg = jax.nn.gelu(y)
        gate = jnp.dot(_mxu(zg), wg_ref[...], preferred_element_type=F32) + bg_ref[...]
        o_ref[b] = (zg * jax.nn.sigmoid(gate)).astype(o_ref.dtype)


def _s5_mixer(x, w_u, wb, wc, a_tab_re, a_tab_im, d_skip, w_glu, b_glu):
    bsz, length, _ = x.shape
    tt = _S5_TT
    n_slab = (_S5_CB // 2) * (S5_DIM // LANES)
    n_pair = n_slab // 2
    full = lambda shape: pl.BlockSpec(shape, lambda t: (0,) * len(shape))
    return pl.pallas_call(
        _s5_kernel,
        out_shape=jax.ShapeDtypeStruct((bsz, length, S5_DIM), _MXU_DTYPE),
        grid=(length // tt,),
        in_specs=[pl.BlockSpec((bsz, tt, D_MODEL), lambda t: (0, t, 0)),
                  full(w_u.shape), full(wb.shape), full(wc.shape), full(a_tab_re.shape), full(a_tab_im.shape),
                  full(d_skip.shape), full(w_glu.shape), full(b_glu.shape)],
        out_specs=pl.BlockSpec((bsz, tt, S5_DIM), lambda t: (0, t, 0)),
        scratch_shapes=[pltpu.VMEM((bsz * tt, S5_DIM), F32),
                        pltpu.VMEM((n_slab, SUBLANES * _S5_PITCH, LANES), F32),
                        pltpu.VMEM((n_pair, SUBLANES, LANES), F32),
                        pltpu.VMEM((n_pair, SUBLANES, LANES), F32)],
        compiler_params=pltpu.CompilerParams(
            dimension_semantics=("arbitrary",), vmem_limit_bytes=_VMEM_LIMIT),
    )(x, w_u, wb, wc, a_tab_re, a_tab_im, d_skip, w_glu, b_glu)


def _s5_tables(abar_re, abar_im, bbar_re, bbar_im, c_re, c_im, bsz):
    gpb = S5_GROUPS // _S5_CB
    eye = jnp.eye(gpb, dtype=F32)

    def drive(bb):
        t = bb.reshape(_S5_CB, gpb, S5_GROUP, S5_STATE)
        return jnp.einsum('cgip,gh->cgihp', t, eye).reshape(_S5_CB, gpb * S5_GROUP, gpb * S5_STATE)

    def readout(cc):
        t = cc.reshape(_S5_CB, gpb, S5_GROUP, S5_STATE)
        return jnp.einsum('cgip,gh->chpgi', t, eye).reshape(_S5_CB, gpb * S5_STATE, gpb * S5_GROUP)

    wb = jnp.concatenate([drive(bbar_re), drive(bbar_im)], axis=2).astype(_MXU_DTYPE)
    wc = jnp.concatenate([readout(c_re), -readout(c_im)], axis=1).astype(_MXU_DTYPE)

    def a_tab(ab):
        flat = ab[::S5_GROUP].reshape(_S5_CB // 2, 2, 1, (gpb * S5_STATE) // LANES, LANES)
        t = jnp.broadcast_to(flat, (_S5_CB // 2, 2, bsz, (gpb * S5_STATE) // LANES, LANES))
        t = jnp.transpose(t, (0, 3, 1, 2, 4))
        return t.reshape((_S5_CB // 2) * ((gpb * S5_STATE) // LANES), 2 * bsz, LANES)

    return wb, wc, a_tab(abar_re), a_tab(abar_im)


def _mixout_kernel(mix_ref, x_ref, wq_ref, k_ref, v_ref, wo_ref, g_ref, b_ref, o_ref):
    xq = jnp.dot(_mxu(x_ref[...]), wq_ref[...], preferred_element_type=F32)
    cross = []
    for hh in range(XA_HEADS):
        sl = slice(hh * HEAD_DIM, (hh + 1) * HEAD_DIM)
        s = _dot_nt(xq[:, sl], k_ref[:, sl]) * (HEAD_DIM ** -0.5)
        e = jnp.exp(s - jnp.max(s, axis=-1, keepdims=True))
        p = e / jnp.sum(e, axis=-1, keepdims=True)
        cross.append(_dot(p, v_ref[:, sl]))
    mix_dim = mix_ref.shape[1]
    h = jnp.dot(_mxu(mix_ref[...]), wo_ref[0:mix_dim, :], preferred_element_type=F32)
    h = h + jnp.dot(_mxu(jnp.concatenate(cross, axis=1)), wo_ref[mix_dim:, :], preferred_element_type=F32)
    o_ref[...] = _layer_norm(DN_ALPHA * x_ref[...] + h, g_ref[...], b_ref[...])


def _mixout(mix, x, w_xq, kv, w_o, ln_g, ln_b, *, bsz, length):
    tm = min(_MIX_TM, length)
    nt = length // tm
    mem_len = kv.shape[0] // bsz
    row = lambda width: pl.BlockSpec((tm, width), lambda b, t: (b * nt + t, 0))
    full = lambda a: pl.BlockSpec(a.shape, lambda b, t: (0,) * a.ndim)
    return pl.pallas_call(
        _mixout_kernel,
        out_shape=jax.ShapeDtypeStruct(x.shape, F32),
        grid=(bsz, nt),
        in_specs=[row(mix.shape[1]), row(D_MODEL), full(w_xq),
                  pl.BlockSpec((mem_len, XA_DIM), lambda b, t: (b, 0)),
                  pl.BlockSpec((mem_len, XA_DIM), lambda b, t: (b, 1)),
                  full(w_o), full(ln_g), full(ln_b)],
        out_specs=row(D_MODEL),
        compiler_params=pltpu.CompilerParams(
            dimension_semantics=("parallel", "parallel"), vmem_limit_bytes=_VMEM_LIMIT),
    )(mix, x, w_xq, kv, kv, w_o, ln_g, ln_b)


def _mlp_kernel(x_ref, w1_ref, w2_ref, g_ref, b_ref, o_ref, acc_ref):
    j = pl.program_id(1)
    hid = jnp.dot(_mxu(x_ref[...]), w1_ref[...], preferred_element_type=F32)
    hid = jnp.square(jnp.maximum(hid, 0.0))
    part = jnp.dot(_mxu(hid), w2_ref[...], preferred_element_type=F32)

    @pl.when(j == 0)
    def _():
        acc_ref[...] = part

    @pl.when(j > 0)
    def _():
        acc_ref[...] += part

    @pl.when(j == pl.num_programs(1) - 1)
    def _():
        o_ref[...] = _layer_norm(DN_ALPHA * x_ref[...] + acc_ref[...], g_ref[...], b_ref[...])


def _mlp(x, w1, w2, ln_g, ln_b):
    m = x.shape[0]
    tm = min(_MLP_TM, m)
    tf = _MLP_TF
    vec = pl.BlockSpec((1, D_MODEL), lambda i, j: (0, 0))
    return pl.pallas_call(
        _mlp_kernel,
        out_shape=jax.ShapeDtypeStruct(x.shape, F32),
        grid=(m // tm, D_FF // tf),
        in_specs=[pl.BlockSpec((tm, D_MODEL), lambda i, j: (i, 0)),
                  pl.BlockSpec((D_MODEL, tf), lambda i, j: (0, j)),
                  pl.BlockSpec((tf, D_MODEL), lambda i, j: (j, 0)), vec, vec],
        out_specs=pl.BlockSpec((tm, D_MODEL), lambda i, j: (i, 0)),
        scratch_shapes=[pltpu.VMEM((tm, D_MODEL), F32)],
        compiler_params=pltpu.CompilerParams(
            dimension_semantics=("parallel", "arbitrary"), vmem_limit_bytes=_VMEM_LIMIT),
    )(x, w1, w2, ln_g, ln_b)


def _row(v, offset=0):
    width = -(-(offset + v.shape[0]) // LANES) * LANES
    return jnp.zeros((1, width), F32).at[0, offset:offset + v.shape[0]].set(v.astype(F32))


def kernel(x, mem, w_kv_mem, w_o, ln1_g, ln1_b, ln2_g, ln2_b, mlp_w1, mlp_w2, gdn_w_in, gdn_conv_w,
           gdn_a_log, gdn_dt_bias, gdn_norm_g, s5_w_in, s5_a_re, s5_a_im, s5_b_re, s5_b_im, s5_c_re,
           s5_c_im, s5_log_dt, s5_d, s5_w_glu, s5_b_glu):
    bsz, length, _ = x.shape
    tokens = bsz * length
    xf = x.reshape(tokens, D_MODEL)
    memf = mem.reshape(bsz * mem.shape[1], D_MODEL)
    qkvz = 4 * GDN_DIM
    for i in range(DEPTH):
        j = i // 2
        kv = _matmul(memf, w_kv_mem[i].astype(_MXU_DTYPE), tm=memf.shape[0], tn=2 * XA_DIM)
        if i % 2 == 0:
            w_in = gdn_w_in[j]
            w_ba = jnp.pad(w_in[:, qkvz:qkvz + 2 * GDN_HEADS], ((0, 0), (0, LANES - 2 * GDN_HEADS)))
            w_xq = w_in[:, qkvz + 2 * GDN_HEADS:]
            mix = _gdn_mixer(xf, w_in[:, :qkvz].astype(_MXU_DTYPE), w_ba.astype(_MXU_DTYPE),
                             gdn_conv_w[j].astype(F32), _row(gdn_a_log[j], GDN_HEADS),
                             _row(gdn_dt_bias[j], GDN_HEADS), _row(gdn_norm_g[j]), bsz=bsz, length=length)
        else:
            w_xq = s5_w_in[j][:, S5_DIM:]
            abar_re, abar_im, bbar_re, bbar_im = _s5_discretize(
                s5_a_re[j].astype(F32), s5_a_im[j].astype(F32), s5_log_dt[j].astype(F32),
                s5_b_re[j].astype(F32), s5_b_im[j].astype(F32))
            wb, wc, a_tab_re, a_tab_im = _s5_tables(
                abar_re, abar_im, bbar_re, bbar_im, s5_c_re[j].astype(F32), s5_c_im[j].astype(F32), bsz)
            mix = _s5_mixer(xf.reshape(bsz, length, D_MODEL), s5_w_in[j][:, :S5_DIM].astype(_MXU_DTYPE),
                            wb, wc, a_tab_re, a_tab_im, _row(s5_d[j]), s5_w_glu[j].astype(_MXU_DTYPE),
                            _row(s5_b_glu[j]))
            mix = mix.reshape(tokens, S5_DIM)
        xf = _mixout(mix, xf, w_xq.astype(_MXU_DTYPE), kv, w_o[i].astype(_MXU_DTYPE), _row(ln1_g[i]),
                     _row(ln1_b[i]), bsz=bsz, length=length)
        xf = _mlp(xf, mlp_w1[i].astype(_MXU_DTYPE), mlp_w2[i].astype(_MXU_DTYPE), _row(ln2_g[i]),
                  _row(ln2_b[i]))
    return xf.reshape(bsz, length, D_MODEL)
```

```python
import functools

import jax
import jax.numpy as jnp
from jax import lax
from jax.experimental import pallas as pl
from jax.experimental.pallas import tpu as pltpu

F32 = jnp.float32
_MXU_DTYPE = jnp.bfloat16

D_MODEL = 1024
DEPTH = 4
GDN_HEADS = 8
HEAD_DIM = 128
GDN_DIM = GDN_HEADS * HEAD_DIM
GDN_CONV = 4
GDN_CHUNK = 64
S5_DIM = D_MODEL
S5_GROUP = 16
S5_GROUPS = S5_DIM // S5_GROUP
S5_STATE = 64
XA_HEADS = 4
XA_DIM = XA_HEADS * HEAD_DIM
D_FF = 4 * D_MODEL
DN_ALPHA = (2 * DEPTH) ** 0.25
LN_EPS = 1e-5
RMS_EPS = 1e-6

LANES = 128
SUBLANES = 8
_VMEM_LIMIT = 56 * 1024 * 1024

_GDN_TT = 256
_GDN_G = 2 * GDN_CHUNK
_S5_TT = 128
_S5_UNROLL = 4
_S5_SCAN_PAIRS = 1
_S5_PITCH = _S5_TT + SUBLANES
_S5_CB = 8
_MIX_TM = 512
_MIX_SPLIT = 2
_MLP_TM = 512
_MLP_SPLIT = 2
_MLP_TF = 1024


def _mxu(a):
    return a.astype(_MXU_DTYPE)


def _dot(a, b):
    return jnp.dot(_mxu(a), _mxu(b), preferred_element_type=F32)


def _dot_nt(a, b):
    return lax.dot_general(_mxu(a), _mxu(b), (((1,), (1,)), ((), ())), preferred_element_type=F32)


def _dot_tn(a, b):
    return lax.dot_general(_mxu(a), _mxu(b), (((0,), (0,)), ((), ())), preferred_element_type=F32)


def _dot_exact_lhs(a01, b):
    hi = b.astype(_MXU_DTYPE)
    r1 = b - hi.astype(F32)
    mid = r1.astype(_MXU_DTYPE)
    lo = (r1 - mid.astype(F32)).astype(_MXU_DTYPE)
    a = a01.astype(_MXU_DTYPE)
    acc = jnp.dot(a, lo, preferred_element_type=F32)
    acc = acc + jnp.dot(a, mid, preferred_element_type=F32)
    return acc + jnp.dot(a, hi, preferred_element_type=F32)


def _layer_norm(y, g, b):
    mu = jnp.mean(y, axis=-1, keepdims=True)
    yc = y - mu
    var = jnp.mean(jnp.square(yc), axis=-1, keepdims=True)
    return yc * lax.rsqrt(var + LN_EPS) * g + b


def _softplus(x):
    return jnp.maximum(x, 0.0) + jnp.log1p(jnp.exp(-jnp.abs(x)))


def _matmul_kernel(x_ref, w_ref, o_ref):
    o_ref[...] = jnp.dot(_mxu(x_ref[...]), w_ref[...], preferred_element_type=F32)


def _matmul(x, w, *, tm, tn):
    m, k = x.shape
    n = w.shape[1]
    return pl.pallas_call(
        _matmul_kernel,
        out_shape=jax.ShapeDtypeStruct((m, n), F32),
        grid=(m // tm, n // tn),
        in_specs=[pl.BlockSpec((tm, k), lambda i, j: (i, 0)),
                  pl.BlockSpec((k, tn), lambda i, j: (0, j))],
        out_specs=pl.BlockSpec((tm, tn), lambda i, j: (i, j)),
        compiler_params=pltpu.CompilerParams(
            dimension_semantics=("parallel", "parallel"), vmem_limit_bytes=_VMEM_LIMIT),
    )(x, w)


def _gdn_kernel(x_ref, w_ref, wba_ref, cw_ref, alog_ref, dtb_ref, ng_ref, o_ref,
                tail_ref, qs_ref, ks_ref, vs_ref, zs_ref, uw_ref, qg_ref, kd_ref, attn_ref,
                el_ref, vn_ref, oi_ref, op_ref, s_ref):
    tt = x_ref.shape[0]
    g_rows = _GDN_G
    c_rows = GDN_CHUNK
    n_groups = tt // g_rows
    cpg = g_rows // c_rows
    vpc = c_rows // SUBLANES
    n_wrap = GDN_CONV - 1

    @pl.when(pl.program_id(1) == 0)
    def _():
        tail_ref[...] = jnp.zeros_like(tail_ref)
        s_ref[...] = jnp.zeros_like(s_ref)

    def time_of(pos):
        local = pos % c_rows
        return (pos // c_rows) * c_rows + local // SUBLANES + SUBLANES * (local % SUBLANES)

    prow = lax.broadcasted_iota(jnp.int32, (tt, tt), 0)
    pcol = lax.broadcasted_iota(jnp.int32, (tt, tt), 1)
    perm = (pcol == time_of(prow)).astype(_MXU_DTYPE)
    xp = jnp.dot(perm, _mxu(x_ref[...]), preferred_element_type=F32).astype(_MXU_DTYPE)

    trow = time_of(lax.broadcasted_iota(jnp.int32, (g_rows, g_rows), 0))
    tcol = time_of(lax.broadcasted_iota(jnp.int32, (g_rows, g_rows), 1))
    same = (trow // c_rows) == (tcol // c_rows)
    causal = same & (trow >= tcol)
    strict = same & (trow > tcol)
    blk16 = (trow // 16) == (tcol // 16)
    blk32 = (trow // 32) == (tcol // 32)
    eye = (trow == tcol).astype(F32)
    ltri = causal.astype(F32)
    lane = lax.broadcasted_iota(jnp.int32, (g_rows, LANES), 1)
    rowi = lax.broadcasted_iota(jnp.int32, (cpg * SUBLANES, GDN_DIM), 0)

    def every(fn, *lists):
        return [fn(*args) for args in zip(*lists)]

    heads = range(GDN_HEADS)
    head_lanes = [slice(hl * HEAD_DIM, (hl + 1) * HEAD_DIM) for hl in heads]
    tails = [[tail_ref[idx, w] for w in range(n_wrap)] for idx in range(3)]
    gates = {}
    states = [s_ref[hl] for hl in heads]


    def project(gi):
        rs = slice(gi * g_rows, (gi + 1) * g_rows)
        xg = xp[rs]
        for idx, dst in enumerate((qs_ref, ks_ref, vs_ref)):
            sec = slice(idx * GDN_DIM, (idx + 1) * GDN_DIM)
            y = jnp.dot(xg, w_ref[:, sec], preferred_element_type=F32)
            yield

            def slab(c, j):
                return y[c * c_rows + j * SUBLANES:c * c_rows + (j + 1) * SUBLANES, :]

            shifted = {}
            for w in range(n_wrap):
                v = vpc - n_wrap + w
                wv = jnp.concatenate([slab(c, v) for c in range(cpg)], axis=0)
                carry = tails[idx][w][SUBLANES - 1:SUBLANES, :]
                shifted[v] = jnp.where(rowi == 0, carry, pltpu.roll(wv, 1, axis=0))
                tails[idx][w] = slab(cpg - 1, v)
            taps = [cw_ref[tap:tap + 1, sec] for tap in range(GDN_CONV)]
            for c in range(cpg):
                for j in range(vpc):
                    def src(d):
                        if j >= d:
                            return slab(c, j - d)
                        return shifted[j - d + vpc][c * SUBLANES:(c + 1) * SUBLANES, :]
                    acc = src(3) * taps[0]
                    for tap in range(1, GDN_CONV):
                        acc = acc + src(GDN_CONV - 1 - tap) * taps[tap]
                    r0 = gi * g_rows + c * c_rows + j * SUBLANES
                    dst[r0:r0 + SUBLANES, :] = acc * jax.nn.sigmoid(acc)
        zs_ref[rs, :] = jnp.dot(xg, w_ref[:, 3 * GDN_DIM:], preferred_element_type=F32)
        yield
        ba = jnp.dot(xg, wba_ref[...], preferred_element_type=F32)
        gates[gi] = (jax.nn.sigmoid(ba), -jnp.exp(alog_ref[...]) * _softplus(ba + dtb_ref[...]))
        yield

    def prepare(gi):
        rs = slice(gi * g_rows, (gi + 1) * g_rows)
        beta_all, g_all = gates.pop(gi)

        def l2n(ref, hs):
            t = ref[rs, hs]
            return t * lax.rsqrt(jnp.sum(jnp.square(t), axis=-1, keepdims=True) + 1e-6)

        def gate_col(t, pos):
            col = jnp.sum(jnp.where(lane == pos, t, 0.0), axis=-1, keepdims=True)
            return jnp.broadcast_to(col, (g_rows, HEAD_DIM))

        q = [l2n(qs_ref, hs) * (HEAD_DIM ** -0.5) for hs in head_lanes]
        k = [l2n(ks_ref, hs) for hs in head_lanes]
        bet = [gate_col(beta_all, hl) for hl in heads]
        gb = [gate_col(g_all, hl + GDN_HEADS) for hl in heads]
        gcb = every(lambda t: _dot_exact_lhs(ltri, t), gb)
        yield
        decay = every(lambda t: jnp.where(causal, jnp.exp(jnp.where(causal, t - t.T, 0.0)), 0.0), gcb)
        kb = every(lambda a, b: a * b, k, bet)
        kk = every(_dot_nt, kb, k)
        yield
        qk = every(_dot_nt, q, k)
        yield
        a_mat = every(lambda m, d: jnp.where(strict, m * d, 0.0), kk, decay)
        for hl, m, d in zip(heads, qk, decay):
            attn_ref[hl, rs, :] = m * d
        d1 = every(lambda a: jnp.where(blk16, a, 0.0), a_mat)
        d2 = every(_dot, d1, d1)
        yield
        d4 = every(_dot, d2, d2)
        yield
        d8 = every(_dot, d4, d4)
        yield
        t_inv = every(lambda d: eye - d, d1)
        for dn in (d2, d4, d8):
            t_inv = every(lambda t, p: t + p, t_inv, every(_dot, t_inv, dn))
            yield
        for off_diag in (lambda a: jnp.where(blk32 & jnp.logical_not(blk16), a, 0.0),
                         lambda a: jnp.where(blk32, 0.0, a)):
            te = every(_dot, t_inv, every(off_diag, a_mat))
            yield
            t_inv = every(lambda t, p: t - p, t_inv, every(_dot, te, t_inv))
            yield
        egc = every(jnp.exp, gcb)
        rhs = [jnp.concatenate([vs_ref[rs, hs] * b, kbi * e], axis=1)
               for hs, b, kbi, e in zip(head_lanes, bet, kb, egc)]
        uw = every(_dot, t_inv, rhs)
        yield
        for hl, uwi, qi, ki, e, g in zip(heads, uw, q, k, egc, gcb):
            uw_ref[hl, rs, :] = uwi
            qg_ref[hl, rs, :] = qi * e
            for c in range(cpg):
                lo = c * c_rows
                hi = lo + c_rows
                g_last = g[hi - 1:hi, :]
                kd_ref[hl, gi * g_rows + lo:gi * g_rows + hi, :] = ki[lo:hi] * jnp.exp(g_last - g[lo:hi])
                ci = gi * cpg + c
                el_ref[hl, ci * SUBLANES:(ci + 1) * SUBLANES, :] = jnp.broadcast_to(
                    jnp.exp(g_last), (SUBLANES, HEAD_DIM))

    def recur(gi):
        for ci in range(gi * cpg, (gi + 1) * cpg):
            cs = slice(ci * c_rows, (ci + 1) * c_rows)
            lhs = [jnp.concatenate([uw_ref[hl, cs, HEAD_DIM:], qg_ref[hl, cs, :]], axis=0) for hl in heads]
            res = every(_dot, lhs, states)
            yield
            vn = [uw_ref[hl, cs, :HEAD_DIM] - res[hl][:c_rows] for hl in heads]
            upd = [_dot_tn(kd_ref[hl, cs, :], vn[hl]) for hl in heads]
            yield
            for hl in heads:
                vn_ref[hl, cs, :] = vn[hl]
                oi_ref[hl, cs, :] = res[hl][c_rows:]
                states[hl] = states[hl] * el_ref[hl, ci * SUBLANES:ci * SUBLANES + 1, :] + upd[hl]

    def finish(gi):
        rs = slice(gi * g_rows, (gi + 1) * g_rows)
        intra = [_dot(attn_ref[hl, rs, :], vn_ref[hl, rs, :]) for hl in heads]
        yield
        for hl, hs in zip(heads, head_lanes):
            o = oi_ref[hl, rs, :] + intra[hl]
            o = o * lax.rsqrt(jnp.mean(jnp.square(o), axis=-1, keepdims=True) + RMS_EPS) * ng_ref[...]
            z = zs_ref[rs, hs]
            op_ref[rs, hs] = (o * (z * jax.nn.sigmoid(z))).astype(op_ref.dtype)

    stage_counts = (5, 14, 2 * cpg, 1)

    def interleave(pipes):
        done = [0] * len(pipes)
        live = set(range(len(pipes)))
        while live:
            i = min(live, key=lambda p: (done[p] + 1) / (pipes[p][1] + 1))
            try:
                next(pipes[i][0])
                done[i] += 1
            except StopIteration:
                live.remove(i)

    stages = (project, prepare, recur, finish)
    for rnd in range(n_groups + len(stages) - 1):
        pipes = [(stage(rnd - lag), stage_counts[lag]) for lag, stage in enumerate(stages)
                 if 0 <= rnd - lag < n_groups]
        interleave(pipes)

    for idx in range(3):
        for w in range(n_wrap):
            tail_ref[idx, w] = tails[idx][w]
    for hl in heads:
        s_ref[hl] = states[hl]
    o_ref[...] = jnp.dot(perm, op_ref[...], preferred_element_type=F32).astype(o_ref.dtype)


def _gdn_mixer(x, w_qkvz, w_ba, conv_w, alog_row, dtb_row, norm_g, *, bsz, length):
    tt = min(_GDN_TT, length)
    nt = length // tt
    full = lambda a: pl.BlockSpec(a.shape, lambda b, t: (0,) * a.ndim)
    per_head = lambda rows, cols: pltpu.VMEM((GDN_HEADS, rows, cols), F32)
    tile = lambda: pltpu.VMEM((tt, GDN_DIM), F32)
    return pl.pallas_call(
        _gdn_kernel,
        out_shape=jax.ShapeDtypeStruct((bsz * length, GDN_DIM), _MXU_DTYPE),
        grid=(bsz, nt),
        in_specs=[pl.BlockSpec((tt, D_MODEL), lambda b, t: (b * nt + t, 0)),
                  full(w_qkvz), full(w_ba), full(conv_w), full(alog_row), full(dtb_row), full(norm_g)],
        out_specs=pl.BlockSpec((tt, GDN_DIM), lambda b, t: (b * nt + t, 0)),
        scratch_shapes=[pltpu.VMEM((3, GDN_CONV - 1, SUBLANES, GDN_DIM), F32),
                        tile(), tile(), tile(), tile(),
                        per_head(tt, 2 * HEAD_DIM), per_head(tt, HEAD_DIM), per_head(tt, HEAD_DIM),
                        per_head(tt, _GDN_G), per_head((tt // GDN_CHUNK) * SUBLANES, HEAD_DIM),
                        per_head(tt, HEAD_DIM), per_head(tt, HEAD_DIM),
                        pltpu.VMEM((tt, GDN_DIM), _MXU_DTYPE),
                        per_head(HEAD_DIM, HEAD_DIM)],
        compiler_params=pltpu.CompilerParams(
            dimension_semantics=("parallel", "arbitrary"), vmem_limit_bytes=_VMEM_LIMIT),
    )(x, w_qkvz, w_ba, conv_w, alog_row, dtb_row, norm_g)


def _s5_discretize_kernel(are_ref, aim_ref, ldt_ref, bre_ref, bim_ref,
                          abre_ref, abim_ref, bbre_ref, bbim_ref):
    a_re = are_ref[...]
    a_im = aim_ref[...]
    dt = jnp.exp(ldt_ref[...])
    mag = jnp.exp(a_re * dt)
    abar_re = mag * jnp.cos(a_im * dt)
    abar_im = mag * jnp.sin(a_im * dt)
    den = jnp.square(a_re) + jnp.square(a_im)
    n_re = abar_re - 1.0
    n_im = abar_im
    f_re = (n_re * a_re + n_im * a_im) / den
    f_im = (n_im * a_re - n_re * a_im) / den
    abre_ref[...] = abar_re
    abim_ref[...] = abar_im
    bbre_ref[...] = f_re * bre_ref[...] - f_im * bim_ref[...]
    bbim_ref[...] = f_re * bim_ref[...] + f_im * bre_ref[...]


def _s5_discretize(a_re, a_im, log_dt, b_re, b_im):
    rows = S5_GROUPS * S5_GROUP
    rep = lambda t: jnp.repeat(t, S5_GROUP, axis=0)
    b2 = lambda t: jnp.transpose(t, (0, 2, 1)).reshape(rows, S5_STATE)
    ldt = jnp.broadcast_to(log_dt[:, None], (S5_GROUPS, S5_STATE))
    shp = jax.ShapeDtypeStruct((rows, S5_STATE), F32)
    return pl.pallas_call(_s5_discretize_kernel, out_shape=(shp, shp, shp, shp))(
        rep(a_re), rep(a_im), rep(ldt), b2(b_re), b2(b_im))


def _s5_kernel(x_ref, wu_ref, wb_ref, wc_ref, are_ref, aim_ref, d_ref, wg_ref, bg_ref, o_ref,
               us_ref, hs_ref, hre_ref, him_ref):
    nb, tt, _ = x_ref.shape
    pitch = _S5_PITCH
    n_lt = S5_DIM // LANES
    half = n_lt // 2

    @pl.when(pl.program_id(0) == 0)
    def _():
        hre_ref[...] = jnp.zeros_like(hre_ref)
        him_ref[...] = jnp.zeros_like(him_ref)

    us_ref[...] = jnp.dot(_mxu(x_ref[...].reshape(nb * tt, D_MODEL)), wu_ref[...],
                          preferred_element_type=F32)
    n_cp = _S5_CB // 2

    def drive(cp):
        for c in (2 * cp, 2 * cp + 1):
            bu = jnp.dot(_mxu(us_ref[:, c * LANES:(c + 1) * LANES]), wb_ref[c], preferred_element_type=F32)
            for b in range(nb):
                s = (c % 2) * nb + b
                for lt in range(n_lt):
                    hs_ref[cp * n_lt + lt, s * pitch:s * pitch + tt, :] = (
                        bu[b * tt:(b + 1) * tt, lt * LANES:(lt + 1) * LANES])

    def scan(cps):
        tiles = [(cp, l) for cp in cps for l in range(half)]
        a_re = [are_ref[cp * half + l] for cp, l in tiles]
        a_im = [aim_ref[cp * half + l] for cp, l in tiles]

        def step(t, carry):
            out = []
            rows = pl.ds(t, SUBLANES, stride=pitch)
            for i, (cp, l) in enumerate(tiles):
                h_re, h_im = carry[2 * i], carry[2 * i + 1]
                sl_re = cp * n_lt + l
                sl_im = cp * n_lt + half + l
                n_re = a_re[i] * h_re - a_im[i] * h_im + hs_ref[sl_re, rows, :]
                n_im = a_re[i] * h_im + a_im[i] * h_re + hs_ref[sl_im, rows, :]
                hs_ref[sl_re, rows, :] = n_re
                hs_ref[sl_im, rows, :] = n_im
                out += [n_re, n_im]
            return tuple(out)

        init = []
        for cp, l in tiles:
            init += [hre_ref[cp * half + l], him_ref[cp * half + l]]
        fin = lax.fori_loop(0, tt, step, tuple(init), unroll=_S5_UNROLL)
        for i, (cp, l) in enumerate(tiles):
            hre_ref[cp * half + l] = fin[2 * i]
            him_ref[cp * half + l] = fin[2 * i + 1]

    def readout(cp):
        out = {}
        for c in (2 * cp, 2 * cp + 1):
            for b in range(nb):
                s = (c % 2) * nb + b
                hcat = jnp.concatenate(
                    [hs_ref[cp * n_lt + lt, s * pitch:s * pitch + tt, :] for lt in range(n_lt)], axis=1)
                out[(b, c)] = jnp.dot(_mxu(hcat), wc_ref[c], preferred_element_type=F32)
        return out

    ys = {}
    for cp in range(n_cp):
        drive(cp)
    for c0 in range(0, n_cp, _S5_SCAN_PAIRS):
        scan(range(c0, c0 + _S5_SCAN_PAIRS))
    for cp in range(n_cp):
        ys.update(readout(cp))

    for b in range(nb):
        y = jnp.concatenate([ys[(b, c)] for c in range(_S5_CB)], axis=1)
        y = y + d_ref[...] * us_ref[b * tt:(b + 1) * tt, :]
        zg = jax.nn.gelu(y)
        gate = jnp.dot(_mxu(zg), wg_ref[...], preferred_element_type=F32) + bg_ref[...]
        o_ref[b] = (zg * jax.nn.sigmoid(gate)).astype(o_ref.dtype)


def _s5_mixer(x, w_u, wb, wc, a_tab_re, a_tab_im, d_skip, w_glu, b_glu):
    bsz, length, _ = x.shape
    tt = _S5_TT
    n_slab = (_S5_CB // 2) * (S5_DIM // LANES)
    n_pair = n_slab // 2
    full = lambda shape: pl.BlockSpec(shape, lambda t: (0,) * len(shape))
    return pl.pallas_call(
        _s5_kernel,
        out_shape=jax.ShapeDtypeStruct((bsz, length, S5_DIM), _MXU_DTYPE),
        grid=(length // tt,),
        in_specs=[pl.BlockSpec((bsz, tt, D_MODEL), lambda t: (0, t, 0)),
                  full(w_u.shape), full(wb.shape), full(wc.shape), full(a_tab_re.shape), full(a_tab_im.shape),
                  full(d_skip.shape), full(w_glu.shape), full(b_glu.shape)],
        out_specs=pl.BlockSpec((bsz, tt, S5_DIM), lambda t: (0, t, 0)),
        scratch_shapes=[pltpu.VMEM((bsz * tt, S5_DIM), F32),
                        pltpu.VMEM((n_slab, SUBLANES * _S5_PITCH, LANES), F32),
                        pltpu.VMEM((n_pair, SUBLANES, LANES), F32),
                        pltpu.VMEM((n_pair, SUBLANES, LANES), F32)],
        compiler_params=pltpu.CompilerParams(
            dimension_semantics=("arbitrary",), vmem_limit_bytes=_VMEM_LIMIT),
    )(x, w_u, wb, wc, a_tab_re, a_tab_im, d_skip, w_glu, b_glu)


def _s5_tables(abar_re, abar_im, bbar_re, bbar_im, c_re, c_im, bsz):
    gpb = S5_GROUPS // _S5_CB
    eye = jnp.eye(gpb, dtype=F32)

    def drive(bb):
        t = bb.reshape(_S5_CB, gpb, S5_GROUP, S5_STATE)
        return jnp.einsum('cgip,gh->cgihp', t, eye).reshape(_S5_CB, gpb * S5_GROUP, gpb * S5_STATE)

    def readout(cc):
        t = cc.reshape(_S5_CB, gpb, S5_GROUP, S5_STATE)
        return jnp.einsum('cgip,gh->chpgi', t, eye).reshape(_S5_CB, gpb * S5_STATE, gpb * S5_GROUP)

    wb = jnp.concatenate([drive(bbar_re), drive(bbar_im)], axis=2).astype(_MXU_DTYPE)
    wc = jnp.concatenate([readout(c_re), -readout(c_im)], axis=1).astype(_MXU_DTYPE)

    def a_tab(ab):
        flat = ab[::S5_GROUP].reshape(_S5_CB // 2, 2, 1, (gpb * S5_STATE) // LANES, LANES)
        t = jnp.broadcast_to(flat, (_S5_CB // 2, 2, bsz, (gpb * S5_STATE) // LANES, LANES))
        t = jnp.transpose(t, (0, 3, 1, 2, 4))
        return t.reshape((_S5_CB // 2) * ((gpb * S5_STATE) // LANES), 2 * bsz, LANES)

    return wb, wc, a_tab(abar_re), a_tab(abar_im)


def _mixout_kernel(mix_ref, x_ref, wq_ref, k_ref, v_ref, wo_ref, g_ref, b_ref, o_ref):
    tm = x_ref.shape[0]
    sub = tm // _MIX_SPLIT
    parts = [slice(s * sub, (s + 1) * sub) for s in range(_MIX_SPLIT)]
    head_lanes = [slice(hh * HEAD_DIM, (hh + 1) * HEAD_DIM) for hh in range(XA_HEADS)]
    mix_dim = mix_ref.shape[1]
    xq = [jnp.dot(_mxu(x_ref[rs, :]), wq_ref[...], preferred_element_type=F32) for rs in parts]
    scores = [[_dot_nt(q[:, sl], k_ref[:, sl]) * (HEAD_DIM ** -0.5) for sl in head_lanes] for q in xq]
    h_mix = [jnp.dot(_mxu(mix_ref[rs, :]), wo_ref[0:mix_dim, :], preferred_element_type=F32) for rs in parts]
    cross = []
    for part_scores in scores:
        outs = []
        for s, sl in zip(part_scores, head_lanes):
            e = jnp.exp(s - jnp.max(s, axis=-1, keepdims=True))
            p = e / jnp.sum(e, axis=-1, keepdims=True)
            outs.append(_dot(p, v_ref[:, sl]))
        cross.append(jnp.concatenate(outs, axis=1))
    for rs, hm, cr in zip(parts, h_mix, cross):
        h = hm + jnp.dot(_mxu(cr), wo_ref[mix_dim:, :], preferred_element_type=F32)
        o_ref[rs, :] = _layer_norm(DN_ALPHA * x_ref[rs, :] + h, g_ref[...], b_ref[...])


def _mixout(mix, x, w_xq, kv, w_o, ln_g, ln_b, *, bsz, length):
    tm = min(_MIX_TM, length)
    nt = length // tm
    mem_len = kv.shape[0] // bsz
    row = lambda width: pl.BlockSpec((tm, width), lambda b, t: (b * nt + t, 0))
    full = lambda a: pl.BlockSpec(a.shape, lambda b, t: (0,) * a.ndim, pipeline_mode=pl.Buffered(1))
    return pl.pallas_call(
        _mixout_kernel,
        out_shape=jax.ShapeDtypeStruct(x.shape, F32),
        grid=(bsz, nt),
        in_specs=[row(mix.shape[1]), row(D_MODEL), full(w_xq),
                  pl.BlockSpec((mem_len, XA_DIM), lambda b, t: (b, 0)),
                  pl.BlockSpec((mem_len, XA_DIM), lambda b, t: (b, 1)),
                  full(w_o), full(ln_g), full(ln_b)],
        out_specs=row(D_MODEL),
        compiler_params=pltpu.CompilerParams(
            dimension_semantics=("parallel", "parallel"), vmem_limit_bytes=_VMEM_LIMIT),
    )(mix, x, w_xq, kv, kv, w_o, ln_g, ln_b)


def _mlp_kernel(x_ref, w1_ref, w2_ref, g_ref, b_ref, o_ref, h_ref):
    tm = x_ref.shape[0]
    sub = tm // _MLP_SPLIT
    for s in range(_MLP_SPLIT):
        rs = slice(s * sub, (s + 1) * sub)
        xb = _mxu(x_ref[rs, :])
        for j in range(D_FF // _MLP_TF):
            fs = slice(j * _MLP_TF, (j + 1) * _MLP_TF)
            hid = jnp.dot(xb, w1_ref[:, fs], preferred_element_type=F32)
            h_ref[rs, fs] = _mxu(jnp.square(jnp.maximum(hid, 0.0)))
        y = jnp.dot(h_ref[rs, :], w2_ref[...], preferred_element_type=F32)
        o_ref[rs, :] = _layer_norm(DN_ALPHA * x_ref[rs, :] + y, g_ref[...], b_ref[...])


def _mlp(x, w1, w2, ln_g, ln_b):
    m = x.shape[0]
    tm = min(_MLP_TM, m)
    const = lambda a: pl.BlockSpec(a.shape, lambda i: (0,) * a.ndim, pipeline_mode=pl.Buffered(1))
    return pl.pallas_call(
        _mlp_kernel,
        out_shape=jax.ShapeDtypeStruct(x.shape, F32),
        grid=(m // tm,),
        in_specs=[pl.BlockSpec((tm, D_MODEL), lambda i: (i, 0)), const(w1), const(w2),
                  const(ln_g), const(ln_b)],
        out_specs=pl.BlockSpec((tm, D_MODEL), lambda i: (i, 0)),
        scratch_shapes=[pltpu.VMEM((tm, D_FF), _MXU_DTYPE)],
        compiler_params=pltpu.CompilerParams(
            dimension_semantics=("parallel",), vmem_limit_bytes=_VMEM_LIMIT),
    )(x, w1, w2, ln_g, ln_b)


def _row(v, offset=0):
    width = -(-(offset + v.shape[0]) // LANES) * LANES
    return jnp.zeros((1, width), F32).at[0, offset:offset + v.shape[0]].set(v.astype(F32))


def kernel(x, mem, w_kv_mem, w_o, ln1_g, ln1_b, ln2_g, ln2_b, mlp_w1, mlp_w2, gdn_w_in, gdn_conv_w,
           gdn_a_log, gdn_dt_bias, gdn_norm_g, s5_w_in, s5_a_re, s5_a_im, s5_b_re, s5_b_im, s5_c_re,
           s5_c_im, s5_log_dt, s5_d, s5_w_glu, s5_b_glu):
    bsz, length, _ = x.shape
    tokens = bsz * length
    xf = x.reshape(tokens, D_MODEL)
    memf = mem.reshape(bsz * mem.shape[1], D_MODEL)
    qkvz = 4 * GDN_DIM
    for i in range(DEPTH):
        j = i // 2
        kv = _matmul(memf, w_kv_mem[i].astype(_MXU_DTYPE), tm=memf.shape[0], tn=2 * XA_DIM)
        if i % 2 == 0:
            w_in = gdn_w_in[j]
            w_ba = jnp.pad(w_in[:, qkvz:qkvz + 2 * GDN_HEADS], ((0, 0), (0, LANES - 2 * GDN_HEADS)))
            w_xq = w_in[:, qkvz + 2 * GDN_HEADS:]
            mix = _gdn_mixer(xf, w_in[:, :qkvz].astype(_MXU_DTYPE), w_ba.astype(_MXU_DTYPE),
                             gdn_conv_w[j].astype(F32), _row(gdn_a_log[j], GDN_HEADS),
                             _row(gdn_dt_bias[j], GDN_HEADS), _row(gdn_norm_g[j]), bsz=bsz, length=length)
        else:
            w_xq = s5_w_in[j][:, S5_DIM:]
            abar_re, abar_im, bbar_re, bbar_im = _s5_discretize(
                s5_a_re[j].astype(F32), s5_a_im[j].astype(F32), s5_log_dt[j].astype(F32),
                s5_b_re[j].astype(F32), s5_b_im[j].astype(F32))
            wb, wc, a_tab_re, a_tab_im = _s5_tables(
                abar_re, abar_im, bbar_re, bbar_im, s5_c_re[j].astype(F32), s5_c_im[j].astype(F32), bsz)
            mix = _s5_mixer(xf.reshape(bsz, length, D_MODEL), s5_w_in[j][:, :S5_DIM].astype(_MXU_DTYPE),
                            wb, wc, a_tab_re, a_tab_im, _row(s5_d[j]), s5_w_glu[j].astype(_MXU_DTYPE),
                            _row(s5_b_glu[j]))
            mix = mix.reshape(tokens, S5_DIM)
        xf = _mixout(mix, xf, w_xq.astype(_MXU_DTYPE), kv, w_o[i].astype(_MXU_DTYPE), _row(ln1_g[i]),
                     _row(ln1_b[i]), bsz=bsz, length=length)
        xf = _mlp(xf, mlp_w1[i].astype(_MXU_DTYPE), mlp_w2[i].astype(_MXU_DTYPE), _row(ln2_g[i]),
                  _row(ln2_b[i]))
    return xf.reshape(bsz, length, D_MODEL)
```

```python
import functools

import jax
import jax.numpy as jnp
from jax import lax
from jax.experimental import pallas as pl
from jax.experimental.pallas import tpu as pltpu

F32 = jnp.float32
_MXU_DTYPE = jnp.bfloat16

D_MODEL = 1024
DEPTH = 4
GDN_HEADS = 8
HEAD_DIM = 128
GDN_DIM = GDN_HEADS * HEAD_DIM
GDN_CONV = 4
GDN_CHUNK = 64
S5_DIM = D_MODEL
S5_GROUP = 16
S5_GROUPS = S5_DIM // S5_GROUP
S5_STATE = 64
XA_HEADS = 4
XA_DIM = XA_HEADS * HEAD_DIM
D_FF = 4 * D_MODEL
DN_ALPHA = (2 * DEPTH) ** 0.25
LN_EPS = 1e-5
RMS_EPS = 1e-6

LANES = 128
SUBLANES = 8
_VMEM_LIMIT = 56 * 1024 * 1024

_GDN_TT = 256
_GDN_G = 2 * GDN_CHUNK
_S5_TT = 128
_S5_UNROLL = 4
_S5_SCAN_PAIRS = 1
_S5_PITCH = _S5_TT + SUBLANES // 2
_S5_CB = 8
_MIX_TM = 512
_MIX_SPLIT = 2
_MLP_SPLIT = 2
_MLP_TF = 1024


def _mxu(a):
    return a.astype(_MXU_DTYPE)


def _dot(a, b):
    return jnp.dot(_mxu(a), _mxu(b), preferred_element_type=F32)


def _dot_nt(a, b):
    return lax.dot_general(_mxu(a), _mxu(b), (((1,), (1,)), ((), ())), preferred_element_type=F32)


def _dot_tn(a, b):
    return lax.dot_general(_mxu(a), _mxu(b), (((0,), (0,)), ((), ())), preferred_element_type=F32)


def _dot_exact_lhs(a01, b):
    hi = b.astype(_MXU_DTYPE)
    r1 = b - hi.astype(F32)
    mid = r1.astype(_MXU_DTYPE)
    lo = (r1 - mid.astype(F32)).astype(_MXU_DTYPE)
    a = a01.astype(_MXU_DTYPE)
    acc = jnp.dot(a, lo, preferred_element_type=F32)
    acc = acc + jnp.dot(a, mid, preferred_element_type=F32)
    return acc + jnp.dot(a, hi, preferred_element_type=F32)


def _layer_norm(y, g, b):
    mu = jnp.mean(y, axis=-1, keepdims=True)
    yc = y - mu
    var = jnp.mean(jnp.square(yc), axis=-1, keepdims=True)
    return yc * lax.rsqrt(var + LN_EPS) * g + b


def _softplus(x):
    return jnp.maximum(x, 0.0) + jnp.log1p(jnp.exp(-jnp.abs(x)))


def _matmul_kernel(x_ref, w_ref, o_ref):
    o_ref[...] = jnp.dot(_mxu(x_ref[...]), w_ref[...], preferred_element_type=F32)


def _matmul(x, w, *, tm, tn):
    m, k = x.shape
    n = w.shape[1]
    return pl.pallas_call(
        _matmul_kernel,
        out_shape=jax.ShapeDtypeStruct((m, n), F32),
        grid=(m // tm, n // tn),
        in_specs=[pl.BlockSpec((tm, k), lambda i, j: (i, 0)),
                  pl.BlockSpec((k, tn), lambda i, j: (0, j))],
        out_specs=pl.BlockSpec((tm, tn), lambda i, j: (i, j)),
        compiler_params=pltpu.CompilerParams(
            dimension_semantics=("parallel", "parallel"), vmem_limit_bytes=_VMEM_LIMIT),
    )(x, w)


def _gdn_kernel(x_ref, w_ref, wba_ref, cw_ref, alog_ref, dtb_ref, ng_ref, o_ref,
                tail_ref, qs_ref, ks_ref, vs_ref, zs_ref, uw_ref, qg_ref, kd_ref, attn_ref,
                el_ref, vn_ref, oi_ref, op_ref, s_ref):
    tt = x_ref.shape[0]
    g_rows = _GDN_G
    c_rows = GDN_CHUNK
    n_groups = tt // g_rows
    cpg = g_rows // c_rows
    vpc = c_rows // SUBLANES
    n_wrap = GDN_CONV - 1

    @pl.when(pl.program_id(1) == 0)
    def _():
        tail_ref[...] = jnp.zeros_like(tail_ref)
        s_ref[...] = jnp.zeros_like(s_ref)

    def time_of(pos):
        local = pos % c_rows
        return (pos // c_rows) * c_rows + local // SUBLANES + SUBLANES * (local % SUBLANES)

    prow = lax.broadcasted_iota(jnp.int32, (tt, tt), 0)
    pcol = lax.broadcasted_iota(jnp.int32, (tt, tt), 1)
    perm = (pcol == time_of(prow)).astype(_MXU_DTYPE)
    xp = jnp.dot(perm, _mxu(x_ref[...]), preferred_element_type=F32).astype(_MXU_DTYPE)

    trow = time_of(lax.broadcasted_iota(jnp.int32, (g_rows, g_rows), 0))
    tcol = time_of(lax.broadcasted_iota(jnp.int32, (g_rows, g_rows), 1))
    same = (trow // c_rows) == (tcol // c_rows)
    causal = same & (trow >= tcol)
    strict = same & (trow > tcol)
    blk16 = (trow // 16) == (tcol // 16)
    blk32 = (trow // 32) == (tcol // 32)
    eye = (trow == tcol).astype(F32)
    ltri = causal.astype(F32)
    lane = lax.broadcasted_iota(jnp.int32, (g_rows, LANES), 1)
    rowi = lax.broadcasted_iota(jnp.int32, (cpg * SUBLANES, GDN_DIM), 0)

    def every(fn, *lists):
        return [fn(*args) for args in zip(*lists)]

    heads = range(GDN_HEADS)
    head_lanes = [slice(hl * HEAD_DIM, (hl + 1) * HEAD_DIM) for hl in heads]
    tails = [[tail_ref[idx, w] for w in range(n_wrap)] for idx in range(3)]
    gates = {}
    states = [s_ref[hl] for hl in heads]


    def project(gi):
        rs = slice(gi * g_rows, (gi + 1) * g_rows)
        xg = xp[rs]
        for idx, dst in enumerate((qs_ref, ks_ref, vs_ref)):
            sec = slice(idx * GDN_DIM, (idx + 1) * GDN_DIM)
            y = jnp.dot(xg, w_ref[:, sec], preferred_element_type=F32)
            yield

            def slab(c, j):
                return y[c * c_rows + j * SUBLANES:c * c_rows + (j + 1) * SUBLANES, :]

            shifted = {}
            for w in range(n_wrap):
                v = vpc - n_wrap + w
                wv = jnp.concatenate([slab(c, v) for c in range(cpg)], axis=0)
                carry = tails[idx][w][SUBLANES - 1:SUBLANES, :]
                shifted[v] = jnp.where(rowi == 0, carry, pltpu.roll(wv, 1, axis=0))
                tails[idx][w] = slab(cpg - 1, v)
            taps = [cw_ref[tap:tap + 1, sec] for tap in range(GDN_CONV)]
            for c in range(cpg):
                for j in range(vpc):
                    def src(d):
                        if j >= d:
                            return slab(c, j - d)
                        return shifted[j - d + vpc][c * SUBLANES:(c + 1) * SUBLANES, :]
                    acc = src(3) * taps[0]
                    for tap in range(1, GDN_CONV):
                        acc = acc + src(GDN_CONV - 1 - tap) * taps[tap]
                    r0 = gi * g_rows + c * c_rows + j * SUBLANES
                    dst[r0:r0 + SUBLANES, :] = acc * jax.nn.sigmoid(acc)
        zs_ref[rs, :] = jnp.dot(xg, w_ref[:, 3 * GDN_DIM:], preferred_element_type=F32)
        yield
        ba = jnp.dot(xg, wba_ref[...], preferred_element_type=F32)
        gates[gi] = (jax.nn.sigmoid(ba), -jnp.exp(alog_ref[...]) * _softplus(ba + dtb_ref[...]))
        yield

    def prepare(gi):
        rs = slice(gi * g_rows, (gi + 1) * g_rows)
        beta_all, g_all = gates.pop(gi)

        def l2n(ref, hs):
            t = ref[rs, hs]
            return t * lax.rsqrt(jnp.sum(jnp.square(t), axis=-1, keepdims=True) + 1e-6)

        def gate_col(t, pos):
            col = jnp.sum(jnp.where(lane == pos, t, 0.0), axis=-1, keepdims=True)
            return jnp.broadcast_to(col, (g_rows, HEAD_DIM))

        q = [l2n(qs_ref, hs) * (HEAD_DIM ** -0.5) for hs in head_lanes]
        k = [l2n(ks_ref, hs) for hs in head_lanes]
        bet = [gate_col(beta_all, hl) for hl in heads]
        gb = [gate_col(g_all, hl + GDN_HEADS) for hl in heads]
        gcb = every(lambda t: _dot_exact_lhs(ltri, t), gb)
        yield
        decay = every(lambda t: jnp.where(causal, jnp.exp(jnp.where(causal, t - t.T, 0.0)), 0.0), gcb)
        kb = every(lambda a, b: a * b, k, bet)
        kk = every(_dot_nt, kb, k)
        yield
        qk = every(_dot_nt, q, k)
        yield
        a_mat = every(lambda m, d: jnp.where(strict, m * d, 0.0), kk, decay)
        for hl, m, d in zip(heads, qk, decay):
            attn_ref[hl, rs, :] = m * d
        d1 = every(lambda a: jnp.where(blk16, a, 0.0), a_mat)
        d2 = every(_dot, d1, d1)
        yield
        d4 = every(_dot, d2, d2)
        yield
        d8 = every(_dot, d4, d4)
        yield
        t_inv = every(lambda d: eye - d, d1)
        for dn in (d2, d4, d8):
            t_inv = every(lambda t, p: t + p, t_inv, every(_dot, t_inv, dn))
            yield
        for off_diag in (lambda a: jnp.where(blk32 & jnp.logical_not(blk16), a, 0.0),
                         lambda a: jnp.where(blk32, 0.0, a)):
            te = every(_dot, t_inv, every(off_diag, a_mat))
            yield
            t_inv = every(lambda t, p: t - p, t_inv, every(_dot, te, t_inv))
            yield
        egc = every(jnp.exp, gcb)
        rhs = [jnp.concatenate([vs_ref[rs, hs] * b, kbi * e], axis=1)
               for hs, b, kbi, e in zip(head_lanes, bet, kb, egc)]
        uw = every(_dot, t_inv, rhs)
        yield
        for hl, uwi, qi, ki, e, g in zip(heads, uw, q, k, egc, gcb):
            uw_ref[hl, rs, :] = uwi
            qg_ref[hl, rs, :] = qi * e
            for c in range(cpg):
                lo = c * c_rows
                hi = lo + c_rows
                g_last = g[hi - 1:hi, :]
                kd_ref[hl, gi * g_rows + lo:gi * g_rows + hi, :] = ki[lo:hi] * jnp.exp(g_last - g[lo:hi])
                ci = gi * cpg + c
                el_ref[hl, ci * SUBLANES:(ci + 1) * SUBLANES, :] = jnp.broadcast_to(
                    jnp.exp(g_last), (SUBLANES, HEAD_DIM))

    def recur(gi):
        for ci in range(gi * cpg, (gi + 1) * cpg):
            cs = slice(ci * c_rows, (ci + 1) * c_rows)
            lhs = [jnp.concatenate([uw_ref[hl, cs, HEAD_DIM:], qg_ref[hl, cs, :]], axis=0) for hl in heads]
            res = every(_dot, lhs, states)
            yield
            vn = [uw_ref[hl, cs, :HEAD_DIM] - res[hl][:c_rows] for hl in heads]
            upd = [_dot_tn(kd_ref[hl, cs, :], vn[hl]) for hl in heads]
            yield
            for hl in heads:
                vn_ref[hl, cs, :] = vn[hl]
                oi_ref[hl, cs, :] = res[hl][c_rows:]
                states[hl] = states[hl] * el_ref[hl, ci * SUBLANES:ci * SUBLANES + 1, :] + upd[hl]

    def finish(gi):
        rs = slice(gi * g_rows, (gi + 1) * g_rows)
        intra = [_dot(attn_ref[hl, rs, :], vn_ref[hl, rs, :]) for hl in heads]
        yield
        for hl, hs in zip(heads, head_lanes):
            o = oi_ref[hl, rs, :] + intra[hl]
            o = o * lax.rsqrt(jnp.mean(jnp.square(o), axis=-1, keepdims=True) + RMS_EPS) * ng_ref[...]
            z = zs_ref[rs, hs]
            op_ref[rs, hs] = (o * (z * jax.nn.sigmoid(z))).astype(op_ref.dtype)

    stage_counts = (5, 14, 2 * cpg, 1)

    def interleave(pipes):
        done = [0] * len(pipes)
        live = set(range(len(pipes)))
        while live:
            i = min(live, key=lambda p: (done[p] + 1) / (pipes[p][1] + 1))
            try:
                next(pipes[i][0])
                done[i] += 1
            except StopIteration:
                live.remove(i)

    stages = (project, prepare, recur, finish)
    for rnd in range(n_groups + len(stages) - 1):
        pipes = [(stage(rnd - lag), stage_counts[lag]) for lag, stage in enumerate(stages)
                 if 0 <= rnd - lag < n_groups]
        interleave(pipes)

    for idx in range(3):
        for w in range(n_wrap):
            tail_ref[idx, w] = tails[idx][w]
    for hl in heads:
        s_ref[hl] = states[hl]
    o_ref[...] = jnp.dot(perm, op_ref[...], preferred_element_type=F32).astype(o_ref.dtype)


def _gdn_mixer(x, w_qkvz, w_ba, conv_w, alog_row, dtb_row, norm_g, *, bsz, length):
    tt = min(_GDN_TT, length)
    nt = length // tt
    full = lambda a: pl.BlockSpec(a.shape, lambda b, t: (0,) * a.ndim)
    per_head = lambda rows, cols: pltpu.VMEM((GDN_HEADS, rows, cols), F32)
    tile = lambda: pltpu.VMEM((tt, GDN_DIM), F32)
    return pl.pallas_call(
        _gdn_kernel,
        out_shape=jax.ShapeDtypeStruct((bsz * length, GDN_DIM), _MXU_DTYPE),
        grid=(bsz, nt),
        in_specs=[pl.BlockSpec((tt, D_MODEL), lambda b, t: (b * nt + t, 0)),
                  full(w_qkvz), full(w_ba), full(conv_w), full(alog_row), full(dtb_row), full(norm_g)],
        out_specs=pl.BlockSpec((tt, GDN_DIM), lambda b, t: (b * nt + t, 0)),
        scratch_shapes=[pltpu.VMEM((3, GDN_CONV - 1, SUBLANES, GDN_DIM), F32),
                        tile(), tile(), tile(), tile(),
                        per_head(tt, 2 * HEAD_DIM), per_head(tt, HEAD_DIM), per_head(tt, HEAD_DIM),
                        per_head(tt, _GDN_G), per_head((tt // GDN_CHUNK) * SUBLANES, HEAD_DIM),
                        per_head(tt, HEAD_DIM), per_head(tt, HEAD_DIM),
                        pltpu.VMEM((tt, GDN_DIM), _MXU_DTYPE),
                        per_head(HEAD_DIM, HEAD_DIM)],
        compiler_params=pltpu.CompilerParams(
            dimension_semantics=("parallel", "arbitrary"), vmem_limit_bytes=_VMEM_LIMIT),
    )(x, w_qkvz, w_ba, conv_w, alog_row, dtb_row, norm_g)


def _s5_discretize_kernel(are_ref, aim_ref, ldt_ref, bre_ref, bim_ref,
                          abre_ref, abim_ref, bbre_ref, bbim_ref):
    a_re = are_ref[...]
    a_im = aim_ref[...]
    dt = jnp.exp(ldt_ref[...])
    mag = jnp.exp(a_re * dt)
    abar_re = mag * jnp.cos(a_im * dt)
    abar_im = mag * jnp.sin(a_im * dt)
    den = jnp.square(a_re) + jnp.square(a_im)
    n_re = abar_re - 1.0
    n_im = abar_im
    f_re = (n_re * a_re + n_im * a_im) / den
    f_im = (n_im * a_re - n_re * a_im) / den
    abre_ref[...] = abar_re
    abim_ref[...] = abar_im
    bbre_ref[...] = f_re * bre_ref[...] - f_im * bim_ref[...]
    bbim_ref[...] = f_re * bim_ref[...] + f_im * bre_ref[...]


def _s5_discretize(a_re, a_im, log_dt, b_re, b_im):
    rows = S5_GROUPS * S5_GROUP
    rep = lambda t: jnp.repeat(t, S5_GROUP, axis=0)
    b2 = lambda t: jnp.transpose(t, (0, 2, 1)).reshape(rows, S5_STATE)
    ldt = jnp.broadcast_to(log_dt[:, None], (S5_GROUPS, S5_STATE))
    shp = jax.ShapeDtypeStruct((rows, S5_STATE), F32)
    return pl.pallas_call(_s5_discretize_kernel, out_shape=(shp, shp, shp, shp))(
        rep(a_re), rep(a_im), rep(ldt), b2(b_re), b2(b_im))


def _s5_kernel(x_ref, wu_ref, wb_ref, wc_ref, are_ref, aim_ref, d_ref, wg_ref, bg_ref, o_ref,
               us_ref, hs_ref, hre_ref, him_ref):
    nb, tt, _ = x_ref.shape
    pitch = _S5_PITCH
    n_lt = S5_DIM // LANES
    half = n_lt // 2

    @pl.when(pl.program_id(0) == 0)
    def _():
        hre_ref[...] = jnp.zeros_like(hre_ref)
        him_ref[...] = jnp.zeros_like(him_ref)

    us_ref[...] = jnp.dot(_mxu(x_ref[...].reshape(nb * tt, D_MODEL)), wu_ref[...],
                          preferred_element_type=F32)
    n_cp = _S5_CB // 2

    def drive(cp):
        for c in (2 * cp, 2 * cp + 1):
            bu = jnp.dot(_mxu(us_ref[:, c * LANES:(c + 1) * LANES]), wb_ref[c], preferred_element_type=F32)
            for b in range(nb):
                s = (c % 2) * nb + b
                for lt in range(n_lt):
                    hs_ref[cp * n_lt + lt, s * pitch:s * pitch + tt, :] = (
                        bu[b * tt:(b + 1) * tt, lt * LANES:(lt + 1) * LANES])

    def scan(cps):
        tiles = [(cp, l) for cp in cps for l in range(half)]
        a_re = [are_ref[cp * half + l] for cp, l in tiles]
        a_im = [aim_ref[cp * half + l] for cp, l in tiles]

        def step(t, carry):
            out = []
            rows = pl.ds(t, SUBLANES, stride=pitch)
            for i, (cp, l) in enumerate(tiles):
                h_re, h_im = carry[2 * i], carry[2 * i + 1]
                sl_re = cp * n_lt + l
                sl_im = cp * n_lt + half + l
                n_re = a_re[i] * h_re - a_im[i] * h_im + hs_ref[sl_re, rows, :]
                n_im = a_re[i] * h_im + a_im[i] * h_re + hs_ref[sl_im, rows, :]
                hs_ref[sl_re, rows, :] = n_re
                hs_ref[sl_im, rows, :] = n_im
                out += [n_re, n_im]
            return tuple(out)

        init = []
        for cp, l in tiles:
            init += [hre_ref[cp * half + l], him_ref[cp * half + l]]
        fin = lax.fori_loop(0, tt, step, tuple(init), unroll=_S5_UNROLL)
        for i, (cp, l) in enumerate(tiles):
            hre_ref[cp * half + l] = fin[2 * i]
            him_ref[cp * half + l] = fin[2 * i + 1]

    def readout(cp):
        out = {}
        for c in (2 * cp, 2 * cp + 1):
            for b in range(nb):
                s = (c % 2) * nb + b
                hcat = jnp.concatenate(
                    [hs_ref[cp * n_lt + lt, s * pitch:s * pitch + tt, :] for lt in range(n_lt)], axis=1)
                out[(b, c)] = jnp.dot(_mxu(hcat), wc_ref[c], preferred_element_type=F32)
        return out

    ys = {}
    for cp in range(n_cp):
        drive(cp)
    for c0 in range(0, n_cp, _S5_SCAN_PAIRS):
        scan(range(c0, c0 + _S5_SCAN_PAIRS))
    for cp in range(n_cp):
        ys.update(readout(cp))

    for b in range(nb):
        y = jnp.concatenate([ys[(b, c)] for c in range(_S5_CB)], axis=1)
        y = y + d_ref[...] * us_ref[b * tt:(b + 1) * tt, :]
        zg = jax.nn.gelu(y)
        gate = jnp.dot(_mxu(zg), wg_ref[...], preferred_element_type=F32) + bg_ref[...]
        o_ref[b] = (zg * jax.nn.sigmoid(gate)).astype(o_ref.dtype)


def _s5_mixer(x, w_u, wb, wc, a_tab_re, a_tab_im, d_skip, w_glu, b_glu):
    bsz, length, _ = x.shape
    tt = _S5_TT
    n_slab = (_S5_CB // 2) * (S5_DIM // LANES)
    n_pair = n_slab // 2
    full = lambda shape: pl.BlockSpec(shape, lambda t: (0,) * len(shape))
    return pl.pallas_call(
        _s5_kernel,
        out_shape=jax.ShapeDtypeStruct((bsz, length, S5_DIM), _MXU_DTYPE),
        grid=(length // tt,),
        in_specs=[pl.BlockSpec((bsz, tt, D_MODEL), lambda t: (0, t, 0)),
                  full(w_u.shape), full(wb.shape), full(wc.shape), full(a_tab_re.shape), full(a_tab_im.shape),
                  full(d_skip.shape), full(w_glu.shape), full(b_glu.shape)],
        out_specs=pl.BlockSpec((bsz, tt, S5_DIM), lambda t: (0, t, 0)),
        scratch_shapes=[pltpu.VMEM((bsz * tt, S5_DIM), F32),
                        pltpu.VMEM((n_slab, SUBLANES * _S5_PITCH, LANES), F32),
                        pltpu.VMEM((n_pair, SUBLANES, LANES), F32),
                        pltpu.VMEM((n_pair, SUBLANES, LANES), F32)],
        compiler_params=pltpu.CompilerParams(
            dimension_semantics=("arbitrary",), vmem_limit_bytes=_VMEM_LIMIT),
    )(x, w_u, wb, wc, a_tab_re, a_tab_im, d_skip, w_glu, b_glu)


def _s5_tables(abar_re, abar_im, bbar_re, bbar_im, c_re, c_im, bsz):
    gpb = S5_GROUPS // _S5_CB
    eye = jnp.eye(gpb, dtype=F32)

    def drive(bb):
        t = bb.reshape(_S5_CB, gpb, S5_GROUP, S5_STATE)
        return jnp.einsum('cgip,gh->cgihp', t, eye).reshape(_S5_CB, gpb * S5_GROUP, gpb * S5_STATE)

    def readout(cc):
        t = cc.reshape(_S5_CB, gpb, S5_GROUP, S5_STATE)
        return jnp.einsum('cgip,gh->chpgi', t, eye).reshape(_S5_CB, gpb * S5_STATE, gpb * S5_GROUP)

    wb = jnp.concatenate([drive(bbar_re), drive(bbar_im)], axis=2).astype(_MXU_DTYPE)
    wc = jnp.concatenate([readout(c_re), -readout(c_im)], axis=1).astype(_MXU_DTYPE)

    def a_tab(ab):
        flat = ab[::S5_GROUP].reshape(_S5_CB // 2, 2, 1, (gpb * S5_STATE) // LANES, LANES)
        t = jnp.broadcast_to(flat, (_S5_CB // 2, 2, bsz, (gpb * S5_STATE) // LANES, LANES))
        t = jnp.transpose(t, (0, 3, 1, 2, 4))
        return t.reshape((_S5_CB // 2) * ((gpb * S5_STATE) // LANES), 2 * bsz, LANES)

    return wb, wc, a_tab(abar_re), a_tab(abar_im)


def _mixout_kernel(mix_ref, x_ref, wq_ref, k_ref, v_ref, wo_ref, g_ref, b_ref, o_ref):
    tm = x_ref.shape[0]
    sub = tm // _MIX_SPLIT
    parts = [slice(s * sub, (s + 1) * sub) for s in range(_MIX_SPLIT)]
    head_lanes = [slice(hh * HEAD_DIM, (hh + 1) * HEAD_DIM) for hh in range(XA_HEADS)]
    mix_dim = mix_ref.shape[1]
    xq = [jnp.dot(_mxu(x_ref[rs, :]), wq_ref[...], preferred_element_type=F32) for rs in parts]
    scores = [[_dot_nt(q[:, sl], k_ref[:, sl]) * (HEAD_DIM ** -0.5) for sl in head_lanes] for q in xq]
    h_mix = [jnp.dot(_mxu(mix_ref[rs, :]), wo_ref[0:mix_dim, :], preferred_element_type=F32) for rs in parts]
    cross = []
    for part_scores in scores:
        outs = []
        for s, sl in zip(part_scores, head_lanes):
            e = jnp.exp(s - jnp.max(s, axis=-1, keepdims=True))
            p = e / jnp.sum(e, axis=-1, keepdims=True)
            outs.append(_dot(p, v_ref[:, sl]))
        cross.append(jnp.concatenate(outs, axis=1))
    for rs, hm, cr in zip(parts, h_mix, cross):
        h = hm + jnp.dot(_mxu(cr), wo_ref[mix_dim:, :], preferred_element_type=F32)
        o_ref[rs, :] = _layer_norm(DN_ALPHA * x_ref[rs, :] + h, g_ref[...], b_ref[...])


def _tail_kernel(mix_ref, x_ref, wq_ref, k_ref, v_ref, wo_ref, g1_ref, b1_ref, w1_ref, w2_ref, g2_ref,
                 b2_ref, o_ref, x1_ref, h_ref):
    _mixout_kernel(mix_ref, x_ref, wq_ref, k_ref, v_ref, wo_ref, g1_ref, b1_ref, x1_ref)
    _mlp_kernel(x1_ref, w1_ref, w2_ref, g2_ref, b2_ref, o_ref, h_ref)


def _layer_tail(mix, x, w_xq, kv, w_o, ln1_g, ln1_b, w1, w2, ln2_g, ln2_b, *, bsz, length):
    tm = min(_MIX_TM, length)
    nt = length // tm
    mem_len = kv.shape[0] // bsz
    row = lambda width: pl.BlockSpec((tm, width), lambda b, t: (b * nt + t, 0))
    full = lambda a: pl.BlockSpec(a.shape, lambda b, t: (0,) * a.ndim, pipeline_mode=pl.Buffered(1))
    return pl.pallas_call(
        _tail_kernel,
        out_shape=jax.ShapeDtypeStruct(x.shape, F32),
        grid=(bsz, nt),
        in_specs=[row(mix.shape[1]), row(D_MODEL), full(w_xq),
                  pl.BlockSpec((mem_len, XA_DIM), lambda b, t: (b, 0)),
                  pl.BlockSpec((mem_len, XA_DIM), lambda b, t: (b, 1)),
                  full(w_o), full(ln1_g), full(ln1_b), full(w1), full(w2), full(ln2_g), full(ln2_b)],
        out_specs=row(D_MODEL),
        scratch_shapes=[pltpu.VMEM((tm, D_MODEL), F32), pltpu.VMEM((tm, D_FF), _MXU_DTYPE)],
        compiler_params=pltpu.CompilerParams(
            dimension_semantics=("parallel", "parallel"), vmem_limit_bytes=_VMEM_LIMIT),
    )(mix, x, w_xq, kv, kv, w_o, ln1_g, ln1_b, w1, w2, ln2_g, ln2_b)


def _mlp_kernel(x_ref, w1_ref, w2_ref, g_ref, b_ref, o_ref, h_ref):
    tm = x_ref.shape[0]
    sub = tm // _MLP_SPLIT
    for s in range(_MLP_SPLIT):
        rs = slice(s * sub, (s + 1) * sub)
        xb = _mxu(x_ref[rs, :])
        for j in range(D_FF // _MLP_TF):
            fs = slice(j * _MLP_TF, (j + 1) * _MLP_TF)
            hid = jnp.dot(xb, w1_ref[:, fs], preferred_element_type=F32)
            h_ref[rs, fs] = _mxu(jnp.square(jnp.maximum(hid, 0.0)))
        y = jnp.dot(h_ref[rs, :], w2_ref[...], preferred_element_type=F32)
        o_ref[rs, :] = _layer_norm(DN_ALPHA * x_ref[rs, :] + y, g_ref[...], b_ref[...])


def _row(v, offset=0):
    width = -(-(offset + v.shape[0]) // LANES) * LANES
    return jnp.zeros((1, width), F32).at[0, offset:offset + v.shape[0]].set(v.astype(F32))


def kernel(x, mem, w_kv_mem, w_o, ln1_g, ln1_b, ln2_g, ln2_b, mlp_w1, mlp_w2, gdn_w_in, gdn_conv_w,
           gdn_a_log, gdn_dt_bias, gdn_norm_g, s5_w_in, s5_a_re, s5_a_im, s5_b_re, s5_b_im, s5_c_re,
           s5_c_im, s5_log_dt, s5_d, s5_w_glu, s5_b_glu):
    bsz, length, _ = x.shape
    tokens = bsz * length
    xf = x.reshape(tokens, D_MODEL)
    memf = mem.reshape(bsz * mem.shape[1], D_MODEL)
    qkvz = 4 * GDN_DIM
    for i in range(DEPTH):
        j = i // 2
        kv = _matmul(memf, w_kv_mem[i].astype(_MXU_DTYPE), tm=memf.shape[0], tn=2 * XA_DIM)
        if i % 2 == 0:
            w_in = gdn_w_in[j]
            w_ba = jnp.pad(w_in[:, qkvz:qkvz + 2 * GDN_HEADS], ((0, 0), (0, LANES - 2 * GDN_HEADS)))
            w_xq = w_in[:, qkvz + 2 * GDN_HEADS:]
            mix = _gdn_mixer(xf, w_in[:, :qkvz].astype(_MXU_DTYPE), w_ba.astype(_MXU_DTYPE),
                             gdn_conv_w[j].astype(F32), _row(gdn_a_log[j], GDN_HEADS),
                             _row(gdn_dt_bias[j], GDN_HEADS), _row(gdn_norm_g[j]), bsz=bsz, length=length)
        else:
            w_xq = s5_w_in[j][:, S5_DIM:]
            abar_re, abar_im, bbar_re, bbar_im = _s5_discretize(
                s5_a_re[j].astype(F32), s5_a_im[j].astype(F32), s5_log_dt[j].astype(F32),
                s5_b_re[j].astype(F32), s5_b_im[j].astype(F32))
            wb, wc, a_tab_re, a_tab_im = _s5_tables(
                abar_re, abar_im, bbar_re, bbar_im, s5_c_re[j].astype(F32), s5_c_im[j].astype(F32), bsz)
            mix = _s5_mixer(xf.reshape(bsz, length, D_MODEL), s5_w_in[j][:, :S5_DIM].astype(_MXU_DTYPE),
                            wb, wc, a_tab_re, a_tab_im, _row(s5_d[j]), s5_w_glu[j].astype(_MXU_DTYPE),
                            _row(s5_b_glu[j]))
            mix = mix.reshape(tokens, S5_DIM)
        xf = _layer_tail(mix, xf, w_xq.astype(_MXU_DTYPE), kv, w_o[i].astype(_MXU_DTYPE), _row(ln1_g[i]),
                         _row(ln1_b[i]), mlp_w1[i].astype(_MXU_DTYPE), mlp_w2[i].astype(_MXU_DTYPE),
                         _row(ln2_g[i]), _row(ln2_b[i]), bsz=bsz, length=length)
    return xf.reshape(bsz, length, D_MODEL)
```

```python
import functools

import jax
import jax.numpy as jnp
from jax import lax
from jax.experimental import pallas as pl
from jax.experimental.pallas import tpu as pltpu

F32 = jnp.float32
_MXU_DTYPE = jnp.bfloat16

D_MODEL = 1024
DEPTH = 4
GDN_HEADS = 8
HEAD_DIM = 128
GDN_DIM = GDN_HEADS * HEAD_DIM
GDN_CONV = 4
GDN_CHUNK = 64
S5_DIM = D_MODEL
S5_GROUP = 16
S5_GROUPS = S5_DIM // S5_GROUP
S5_STATE = 64
XA_HEADS = 4
XA_DIM = XA_HEADS * HEAD_DIM
D_FF = 4 * D_MODEL
DN_ALPHA = (2 * DEPTH) ** 0.25
LN_EPS = 1e-5
RMS_EPS = 1e-6

LANES = 128
SUBLANES = 8
_VMEM_LIMIT = 56 * 1024 * 1024

_GDN_TT = 256
_GDN_G = 2 * GDN_CHUNK
_GDN_HEAD_BATCH = 8
_S5_TT = 128
_S5_UNROLL = 4
_S5_SCAN_PAIRS = 1
_S5_PITCH = _S5_TT + SUBLANES // 2
_S5_CB = 8
_MIX_TM = 512
_MIX_SPLIT = 2
_MLP_SPLIT = 2
_MLP_TF = 1024


def _mxu(a):
    return a.astype(_MXU_DTYPE)


def _dot(a, b):
    return jnp.dot(_mxu(a), _mxu(b), preferred_element_type=F32)


def _dot_nt(a, b):
    return lax.dot_general(_mxu(a), _mxu(b), (((1,), (1,)), ((), ())), preferred_element_type=F32)


def _dot_tn(a, b):
    return lax.dot_general(_mxu(a), _mxu(b), (((0,), (0,)), ((), ())), preferred_element_type=F32)


def _dot_exact_lhs(a01, b):
    hi = b.astype(_MXU_DTYPE)
    r1 = b - hi.astype(F32)
    mid = r1.astype(_MXU_DTYPE)
    lo = (r1 - mid.astype(F32)).astype(_MXU_DTYPE)
    a = a01.astype(_MXU_DTYPE)
    acc = jnp.dot(a, lo, preferred_element_type=F32)
    acc = acc + jnp.dot(a, mid, preferred_element_type=F32)
    return acc + jnp.dot(a, hi, preferred_element_type=F32)


def _layer_norm(y, g, b):
    mu = jnp.mean(y, axis=-1, keepdims=True)
    yc = y - mu
    var = jnp.mean(jnp.square(yc), axis=-1, keepdims=True)
    return yc * lax.rsqrt(var + LN_EPS) * g + b


def _softplus(x):
    return jnp.maximum(x, 0.0) + jnp.log1p(jnp.exp(-jnp.abs(x)))


def _matmul_kernel(x_ref, w_ref, o_ref):
    o_ref[...] = jnp.dot(_mxu(x_ref[...]), w_ref[...], preferred_element_type=F32)


def _matmul(x, w, *, tm, tn):
    m, k = x.shape
    n = w.shape[1]
    return pl.pallas_call(
        _matmul_kernel,
        out_shape=jax.ShapeDtypeStruct((m, n), F32),
        grid=(m // tm, n // tn),
        in_specs=[pl.BlockSpec((tm, k), lambda i, j: (i, 0)),
                  pl.BlockSpec((k, tn), lambda i, j: (0, j))],
        out_specs=pl.BlockSpec((tm, tn), lambda i, j: (i, j)),
        compiler_params=pltpu.CompilerParams(
            dimension_semantics=("parallel", "parallel"), vmem_limit_bytes=_VMEM_LIMIT),
    )(x, w)


def _gdn_kernel(x_ref, w_ref, wba_ref, cw_ref, alog_ref, dtb_ref, ng_ref, o_ref,
                tail_ref, qs_ref, ks_ref, vs_ref, zs_ref, uw_ref, qg_ref, kd_ref, attn_ref,
                el_ref, vn_ref, oi_ref, op_ref, s_ref):
    tt = x_ref.shape[0]
    g_rows = _GDN_G
    c_rows = GDN_CHUNK
    n_groups = tt // g_rows
    cpg = g_rows // c_rows
    vpc = c_rows // SUBLANES
    n_wrap = GDN_CONV - 1

    @pl.when(pl.program_id(1) == 0)
    def _():
        tail_ref[...] = jnp.zeros_like(tail_ref)
        s_ref[...] = jnp.zeros_like(s_ref)

    def time_of(pos):
        local = pos % c_rows
        return (pos // c_rows) * c_rows + local // SUBLANES + SUBLANES * (local % SUBLANES)

    prow = lax.broadcasted_iota(jnp.int32, (tt, tt), 0)
    pcol = lax.broadcasted_iota(jnp.int32, (tt, tt), 1)
    perm = (pcol == time_of(prow)).astype(_MXU_DTYPE)
    xp = jnp.dot(perm, _mxu(x_ref[...]), preferred_element_type=F32).astype(_MXU_DTYPE)

    trow = time_of(lax.broadcasted_iota(jnp.int32, (g_rows, g_rows), 0))
    tcol = time_of(lax.broadcasted_iota(jnp.int32, (g_rows, g_rows), 1))
    same = (trow // c_rows) == (tcol // c_rows)
    causal = same & (trow >= tcol)
    strict = same & (trow > tcol)
    blk16 = (trow // 16) == (tcol // 16)
    blk32 = (trow // 32) == (tcol // 32)
    eye = (trow == tcol).astype(F32)
    ltri = causal.astype(F32)
    lane = lax.broadcasted_iota(jnp.int32, (g_rows, LANES), 1)
    rowi = lax.broadcasted_iota(jnp.int32, (cpg * SUBLANES, GDN_DIM), 0)

    def every(fn, *lists):
        return [fn(*args) for args in zip(*lists)]

    heads = range(GDN_HEADS)
    head_lanes = [slice(hl * HEAD_DIM, (hl + 1) * HEAD_DIM) for hl in heads]
    tails = [[tail_ref[idx, w] for w in range(n_wrap)] for idx in range(3)]
    gates = {}
    projected = {}
    states = [s_ref[hl] for hl in heads]


    def project(gi):
        rs = slice(gi * g_rows, (gi + 1) * g_rows)
        for idx, dst in enumerate((qs_ref, ks_ref, vs_ref)):
            sec = slice(idx * GDN_DIM, (idx + 1) * GDN_DIM)
            if gi == 0:
                projected[idx] = jnp.dot(xp, w_ref[:, sec], preferred_element_type=F32)
            y = projected[idx]
            yield

            def slab(c, j):
                r0 = gi * g_rows + c * c_rows + j * SUBLANES
                return y[r0:r0 + SUBLANES, :]

            shifted = {}
            for w in range(n_wrap):
                v = vpc - n_wrap + w
                wv = jnp.concatenate([slab(c, v) for c in range(cpg)], axis=0)
                carry = tails[idx][w][SUBLANES - 1:SUBLANES, :]
                shifted[v] = jnp.where(rowi == 0, carry, pltpu.roll(wv, 1, axis=0))
                tails[idx][w] = slab(cpg - 1, v)
            taps = [cw_ref[tap:tap + 1, sec] for tap in range(GDN_CONV)]
            for c in range(cpg):
                for j in range(vpc):
                    def src(d):
                        if j >= d:
                            return slab(c, j - d)
                        return shifted[j - d + vpc][c * SUBLANES:(c + 1) * SUBLANES, :]
                    acc = src(3) * taps[0]
                    for tap in range(1, GDN_CONV):
                        acc = acc + src(GDN_CONV - 1 - tap) * taps[tap]
                    r0 = gi * g_rows + c * c_rows + j * SUBLANES
                    dst[r0:r0 + SUBLANES, :] = acc * jax.nn.sigmoid(acc)
        if gi == 0:
            zs_ref[...] = jnp.dot(xp, w_ref[:, 3 * GDN_DIM:], preferred_element_type=F32)
            projected[3] = jnp.dot(xp, wba_ref[...], preferred_element_type=F32)
        yield
        ba = projected[3][rs]
        gates[gi] = (jax.nn.sigmoid(ba), -jnp.exp(alog_ref[...]) * _softplus(ba + dtb_ref[...]))
        yield

    def prepare(gi):
        beta_all, g_all = gates.pop(gi)
        gc_all = _dot_exact_lhs(ltri, g_all)
        yield
        for h0 in range(0, GDN_HEADS, _GDN_HEAD_BATCH):
            yield from prepare_heads(gi, range(h0, h0 + _GDN_HEAD_BATCH), beta_all, gc_all)

    def prepare_heads(gi, heads, beta_all, gc_all):
        rs = slice(gi * g_rows, (gi + 1) * g_rows)
        head_lanes = [slice(hl * HEAD_DIM, (hl + 1) * HEAD_DIM) for hl in heads]

        def l2n(ref, hs):
            t = ref[rs, hs]
            return t * lax.rsqrt(jnp.sum(jnp.square(t), axis=-1, keepdims=True) + 1e-6)

        def gate_col(t, pos):
            col = jnp.sum(jnp.where(lane == pos, t, 0.0), axis=-1, keepdims=True)
            return jnp.broadcast_to(col, (g_rows, HEAD_DIM))

        q = [l2n(qs_ref, hs) * (HEAD_DIM ** -0.5) for hs in head_lanes]
        k = [l2n(ks_ref, hs) for hs in head_lanes]
        bet = [gate_col(beta_all, hl) for hl in heads]
        gcb = [gate_col(gc_all, hl + GDN_HEADS) for hl in heads]
        decay = every(lambda t: jnp.where(causal, jnp.exp(jnp.where(causal, t - t.T, 0.0)), 0.0), gcb)
        kb = every(lambda a, b: a * b, k, bet)
        kk = every(_dot_nt, kb, k)
        yield
        qk = every(_dot_nt, q, k)
        yield
        a_mat = every(lambda m, d: jnp.where(strict, m * d, 0.0), kk, decay)
        for hl, m, d in zip(heads, qk, decay):
            attn_ref[hl, rs, :] = m * d
        d1 = every(lambda a: jnp.where(blk16, a, 0.0), a_mat)
        d2 = every(_dot, d1, d1)
        yield
        d4 = every(_dot, d2, d2)
        yield
        d8 = every(_dot, d4, d4)
        yield
        t_inv = every(lambda d: eye - d, d1)
        for dn in (d2, d4, d8):
            t_inv = every(lambda t, p: t + p, t_inv, every(_dot, t_inv, dn))
            yield
        for off_diag in (lambda a: jnp.where(blk32 & jnp.logical_not(blk16), a, 0.0),
                         lambda a: jnp.where(blk32, 0.0, a)):
            te = every(_dot, t_inv, every(off_diag, a_mat))
            yield
            t_inv = every(lambda t, p: t - p, t_inv, every(_dot, te, t_inv))
            yield
        egc = every(jnp.exp, gcb)
        rhs = [jnp.concatenate([vs_ref[rs, hs] * b, kbi * e], axis=1)
               for hs, b, kbi, e in zip(head_lanes, bet, kb, egc)]
        uw = every(_dot, t_inv, rhs)
        yield
        for hl, uwi, qi, ki, e, g in zip(heads, uw, q, k, egc, gcb):
            uw_ref[hl, rs, :] = uwi
            qg_ref[hl, rs, :] = qi * e
            for c in range(cpg):
                lo = c * c_rows
                hi = lo + c_rows
                g_last = g[hi - 1:hi, :]
                kd_ref[hl, gi * g_rows + lo:gi * g_rows + hi, :] = ki[lo:hi] * jnp.exp(g_last - g[lo:hi])
                ci = gi * cpg + c
                el_ref[hl, ci * SUBLANES:(ci + 1) * SUBLANES, :] = jnp.broadcast_to(
                    jnp.exp(g_last), (SUBLANES, HEAD_DIM))

    def recur(gi):
        for ci in range(gi * cpg, (gi + 1) * cpg):
            cs = slice(ci * c_rows, (ci + 1) * c_rows)
            lhs = [jnp.concatenate([uw_ref[hl, cs, HEAD_DIM:], qg_ref[hl, cs, :]], axis=0) for hl in heads]
            res = every(_dot, lhs, states)
            yield
            vn = [uw_ref[hl, cs, :HEAD_DIM] - res[hl][:c_rows] for hl in heads]
            upd = [_dot_tn(kd_ref[hl, cs, :], vn[hl]) for hl in heads]
            yield
            for hl in heads:
                vn_ref[hl, cs, :] = vn[hl]
                oi_ref[hl, cs, :] = res[hl][c_rows:]
                states[hl] = states[hl] * el_ref[hl, ci * SUBLANES:ci * SUBLANES + 1, :] + upd[hl]

    def finish(gi):
        rs = slice(gi * g_rows, (gi + 1) * g_rows)
        intra = [_dot(attn_ref[hl, rs, :], vn_ref[hl, rs, :]) for hl in heads]
        yield
        for hl, hs in zip(heads, head_lanes):
            o = oi_ref[hl, rs, :] + intra[hl]
            o = o * lax.rsqrt(jnp.mean(jnp.square(o), axis=-1, keepdims=True) + RMS_EPS) * ng_ref[...]
            z = zs_ref[rs, hs]
            op_ref[rs, hs] = (o * (z * jax.nn.sigmoid(z))).astype(op_ref.dtype)

    stage_counts = (5, 1 + 13 * (GDN_HEADS // _GDN_HEAD_BATCH), 2 * cpg, 1)

    def interleave(pipes):
        done = [0] * len(pipes)
        live = set(range(len(pipes)))
        while live:
            i = min(live, key=lambda p: (done[p] + 1) / (pipes[p][1] + 1))
            try:
                next(pipes[i][0])
                done[i] += 1
            except StopIteration:
                live.remove(i)

    stages = (project, prepare, recur, finish)
    for rnd in range(n_groups + len(stages) - 1):
        pipes = [(stage(rnd - lag), stage_counts[lag]) for lag, stage in enumerate(stages)
                 if 0 <= rnd - lag < n_groups]
        interleave(pipes)

    for idx in range(3):
        for w in range(n_wrap):
            tail_ref[idx, w] = tails[idx][w]
    for hl in heads:
        s_ref[hl] = states[hl]
    o_ref[...] = jnp.dot(perm, op_ref[...], preferred_element_type=F32).astype(o_ref.dtype)


def _gdn_mixer(x, w_qkvz, w_ba, conv_w, alog_row, dtb_row, norm_g, *, bsz, length):
    tt = min(_GDN_TT, length)
    nt = length // tt
    full = lambda a: pl.BlockSpec(a.shape, lambda b, t: (0,) * a.ndim)
    per_head = lambda rows, cols: pltpu.VMEM((GDN_HEADS, rows, cols), F32)
    tile = lambda: pltpu.VMEM((tt, GDN_DIM), F32)
    return pl.pallas_call(
        _gdn_kernel,
        out_shape=jax.ShapeDtypeStruct((bsz * length, GDN_DIM), _MXU_DTYPE),
        grid=(bsz, nt),
        in_specs=[pl.BlockSpec((tt, D_MODEL), lambda b, t: (b * nt + t, 0)),
                  full(w_qkvz), full(w_ba), full(conv_w), full(alog_row), full(dtb_row), full(norm_g)],
        out_specs=pl.BlockSpec((tt, GDN_DIM), lambda b, t: (b * nt + t, 0)),
        scratch_shapes=[pltpu.VMEM((3, GDN_CONV - 1, SUBLANES, GDN_DIM), F32),
                        tile(), tile(), tile(), tile(),
                        per_head(tt, 2 * HEAD_DIM), per_head(tt, HEAD_DIM), per_head(tt, HEAD_DIM),
                        per_head(tt, _GDN_G), per_head((tt // GDN_CHUNK) * SUBLANES, HEAD_DIM),
                        per_head(tt, HEAD_DIM), per_head(tt, HEAD_DIM),
                        pltpu.VMEM((tt, GDN_DIM), _MXU_DTYPE),
                        per_head(HEAD_DIM, HEAD_DIM)],
        compiler_params=pltpu.CompilerParams(
            dimension_semantics=("parallel", "arbitrary"), vmem_limit_bytes=_VMEM_LIMIT),
    )(x, w_qkvz, w_ba, conv_w, alog_row, dtb_row, norm_g)


def _s5_discretize_kernel(are_ref, aim_ref, ldt_ref, bre_ref, bim_ref,
                          abre_ref, abim_ref, bbre_ref, bbim_ref):
    a_re = are_ref[...]
    a_im = aim_ref[...]
    dt = jnp.exp(ldt_ref[...])
    mag = jnp.exp(a_re * dt)
    abar_re = mag * jnp.cos(a_im * dt)
    abar_im = mag * jnp.sin(a_im * dt)
    den = jnp.square(a_re) + jnp.square(a_im)
    n_re = abar_re - 1.0
    n_im = abar_im
    f_re = (n_re * a_re + n_im * a_im) / den
    f_im = (n_im * a_re - n_re * a_im) / den
    abre_ref[...] = abar_re
    abim_ref[...] = abar_im
    bbre_ref[...] = f_re * bre_ref[...] - f_im * bim_ref[...]
    bbim_ref[...] = f_re * bim_ref[...] + f_im * bre_ref[...]


def _s5_discretize(a_re, a_im, log_dt, b_re, b_im):
    rows = S5_GROUPS * S5_GROUP
    rep = lambda t: jnp.repeat(t, S5_GROUP, axis=0)
    b2 = lambda t: jnp.transpose(t, (0, 2, 1)).reshape(rows, S5_STATE)
    ldt = jnp.broadcast_to(log_dt[:, None], (S5_GROUPS, S5_STATE))
    shp = jax.ShapeDtypeStruct((rows, S5_STATE), F32)
    return pl.pallas_call(_s5_discretize_kernel, out_shape=(shp, shp, shp, shp))(
        rep(a_re), rep(a_im), rep(ldt), b2(b_re), b2(b_im))


def _s5_kernel(x_ref, wu_ref, wb_ref, wc_ref, are_ref, aim_ref, d_ref, wg_ref, bg_ref, o_ref,
               us_ref, hs_ref, hre_ref, him_ref):
    nb, tt, _ = x_ref.shape
    pitch = _S5_PITCH
    n_lt = S5_DIM // LANES
    half = n_lt // 2

    @pl.when(pl.program_id(0) == 0)
    def _():
        hre_ref[...] = jnp.zeros_like(hre_ref)
        him_ref[...] = jnp.zeros_like(him_ref)

    us_ref[...] = jnp.dot(_mxu(x_ref[...].reshape(nb * tt, D_MODEL)), wu_ref[...],
                          preferred_element_type=F32)
    n_cp = _S5_CB // 2

    def drive(cp):
        for c in (2 * cp, 2 * cp + 1):
            bu = jnp.dot(_mxu(us_ref[:, c * LANES:(c + 1) * LANES]), wb_ref[c], preferred_element_type=F32)
            for b in range(nb):
                s = (c % 2) * nb + b
                for lt in range(n_lt):
                    hs_ref[cp * n_lt + lt, s * pitch:s * pitch + tt, :] = (
                        bu[b * tt:(b + 1) * tt, lt * LANES:(lt + 1) * LANES])

    def scan(cps):
        tiles = [(cp, l) for cp in cps for l in range(half)]
        a_re = [are_ref[cp * half + l] for cp, l in tiles]
        a_im = [aim_ref[cp * half + l] for cp, l in tiles]

        def step(t, carry):
            out = []
            rows = pl.ds(t, SUBLANES, stride=pitch)
            for i, (cp, l) in enumerate(tiles):
                h_re, h_im = carry[2 * i], carry[2 * i + 1]
                sl_re = cp * n_lt + l
                sl_im = cp * n_lt + half + l
                n_re = a_re[i] * h_re - a_im[i] * h_im + hs_ref[sl_re, rows, :]
                n_im = a_re[i] * h_im + a_im[i] * h_re + hs_ref[sl_im, rows, :]
                hs_ref[sl_re, rows, :] = n_re
                hs_ref[sl_im, rows, :] = n_im
                out += [n_re, n_im]
            return tuple(out)

        init = []
        for cp, l in tiles:
            init += [hre_ref[cp * half + l], him_ref[cp * half + l]]
        fin = lax.fori_loop(0, tt, step, tuple(init), unroll=_S5_UNROLL)
        for i, (cp, l) in enumerate(tiles):
            hre_ref[cp * half + l] = fin[2 * i]
            him_ref[cp * half + l] = fin[2 * i + 1]

    def readout(cp):
        out = {}
        for c in (2 * cp, 2 * cp + 1):
            for b in range(nb):
                s = (c % 2) * nb + b
                hcat = jnp.concatenate(
                    [hs_ref[cp * n_lt + lt, s * pitch:s * pitch + tt, :] for lt in range(n_lt)], axis=1)
                out[(b, c)] = jnp.dot(_mxu(hcat), wc_ref[c], preferred_element_type=F32)
        return out

    ys = {}
    for cp in range(n_cp):
        drive(cp)
    for c0 in range(0, n_cp, _S5_SCAN_PAIRS):
        scan(range(c0, c0 + _S5_SCAN_PAIRS))
    for cp in range(n_cp):
        ys.update(readout(cp))

    for b in range(nb):
        y = jnp.concatenate([ys[(b, c)] for c in range(_S5_CB)], axis=1)
        y = y + d_ref[...] * us_ref[b * tt:(b + 1) * tt, :]
        zg = jax.nn.gelu(y)
        gate = jnp.dot(_mxu(zg), wg_ref[...], preferred_element_type=F32) + bg_ref[...]
        o_ref[b] = (zg * jax.nn.sigmoid(gate)).astype(o_ref.dtype)


def _s5_mixer(x, w_u, wb, wc, a_tab_re, a_tab_im, d_skip, w_glu, b_glu):
    bsz, length, _ = x.shape
    tt = _S5_TT
    n_slab = (_S5_CB // 2) * (S5_DIM // LANES)
    n_pair = n_slab // 2
    full = lambda shape: pl.BlockSpec(shape, lambda t: (0,) * len(shape))
    return pl.pallas_call(
        _s5_kernel,
        out_shape=jax.ShapeDtypeStruct((bsz, length, S5_DIM), _MXU_DTYPE),
        grid=(length // tt,),
        in_specs=[pl.BlockSpec((bsz, tt, D_MODEL), lambda t: (0, t, 0)),
                  full(w_u.shape), full(wb.shape), full(wc.shape), full(a_tab_re.shape), full(a_tab_im.shape),
                  full(d_skip.shape), full(w_glu.shape), full(b_glu.shape)],
        out_specs=pl.BlockSpec((bsz, tt, S5_DIM), lambda t: (0, t, 0)),
        scratch_shapes=[pltpu.VMEM((bsz * tt, S5_DIM), F32),
                        pltpu.VMEM((n_slab, SUBLANES * _S5_PITCH, LANES), F32),
                        pltpu.VMEM((n_pair, SUBLANES, LANES), F32),
                        pltpu.VMEM((n_pair, SUBLANES, LANES), F32)],
        compiler_params=pltpu.CompilerParams(
            dimension_semantics=("arbitrary",), vmem_limit_bytes=_VMEM_LIMIT),
    )(x, w_u, wb, wc, a_tab_re, a_tab_im, d_skip, w_glu, b_glu)


def _s5_tables(abar_re, abar_im, bbar_re, bbar_im, c_re, c_im, bsz):
    gpb = S5_GROUPS // _S5_CB
    eye = jnp.eye(gpb, dtype=F32)

    def drive(bb):
        t = bb.reshape(_S5_CB, gpb, S5_GROUP, S5_STATE)
        return jnp.einsum('cgip,gh->cgihp', t, eye).reshape(_S5_CB, gpb * S5_GROUP, gpb * S5_STATE)

    def readout(cc):
        t = cc.reshape(_S5_CB, gpb, S5_GROUP, S5_STATE)
        return jnp.einsum('cgip,gh->chpgi', t, eye).reshape(_S5_CB, gpb * S5_STATE, gpb * S5_GROUP)

    wb = jnp.concatenate([drive(bbar_re), drive(bbar_im)], axis=2).astype(_MXU_DTYPE)
    wc = jnp.concatenate([readout(c_re), -readout(c_im)], axis=1).astype(_MXU_DTYPE)

    def a_tab(ab):
        flat = ab[::S5_GROUP].reshape(_S5_CB // 2, 2, 1, (gpb * S5_STATE) // LANES, LANES)
        t = jnp.broadcast_to(flat, (_S5_CB // 2, 2, bsz, (gpb * S5_STATE) // LANES, LANES))
        t = jnp.transpose(t, (0, 3, 1, 2, 4))
        return t.reshape((_S5_CB // 2) * ((gpb * S5_STATE) // LANES), 2 * bsz, LANES)

    return wb, wc, a_tab(abar_re), a_tab(abar_im)


def _mixout_kernel(mix_ref, x_ref, wq_ref, k_ref, v_ref, wo_ref, g_ref, b_ref, o_ref):
    tm = x_ref.shape[0]
    sub = tm // _MIX_SPLIT
    parts = [slice(s * sub, (s + 1) * sub) for s in range(_MIX_SPLIT)]
    head_lanes = [slice(hh * HEAD_DIM, (hh + 1) * HEAD_DIM) for hh in range(XA_HEADS)]
    mix_dim = mix_ref.shape[1]
    xq = [jnp.dot(_mxu(x_ref[rs, :]), wq_ref[...], preferred_element_type=F32) for rs in parts]
    scores = [[_dot_nt(q[:, sl], k_ref[:, sl]) * (HEAD_DIM ** -0.5) for sl in head_lanes] for q in xq]
    h_mix = [jnp.dot(_mxu(mix_ref[rs, :]), wo_ref[0:mix_dim, :], preferred_element_type=F32) for rs in parts]
    cross = []
    for part_scores in scores:
        outs = []
        for s, sl in zip(part_scores, head_lanes):
            e = jnp.exp(s - jnp.max(s, axis=-1, keepdims=True))
            p = e / jnp.sum(e, axis=-1, keepdims=True)
            outs.append(_dot(p, v_ref[:, sl]))
        cross.append(jnp.concatenate(outs, axis=1))
    for rs, hm, cr in zip(parts, h_mix, cross):
        h = hm + jnp.dot(_mxu(cr), wo_ref[mix_dim:, :], preferred_element_type=F32)
        o_ref[rs, :] = _layer_norm(DN_ALPHA * x_ref[rs, :] + h, g_ref[...], b_ref[...])


def _tail_kernel(mix_ref, x_ref, wq_ref, k_ref, v_ref, wo_ref, g1_ref, b1_ref, w1_ref, w2_ref, g2_ref,
                 b2_ref, o_ref, x1_ref, h_ref):
    _mixout_kernel(mix_ref, x_ref, wq_ref, k_ref, v_ref, wo_ref, g1_ref, b1_ref, x1_ref)
    _mlp_kernel(x1_ref, w1_ref, w2_ref, g2_ref, b2_ref, o_ref, h_ref)


def _layer_tail(mix, x, w_xq, kv, w_o, ln1_g, ln1_b, w1, w2, ln2_g, ln2_b, *, bsz, length):
    tm = min(_MIX_TM, length)
    nt = length // tm
    mem_len = kv.shape[0] // bsz
    row = lambda width: pl.BlockSpec((tm, width), lambda b, t: (b * nt + t, 0))
    full = lambda a: pl.BlockSpec(a.shape, lambda b, t: (0,) * a.ndim, pipeline_mode=pl.Buffered(1))
    return pl.pallas_call(
        _tail_kernel,
        out_shape=jax.ShapeDtypeStruct(x.shape, F32),
        grid=(bsz, nt),
        in_specs=[row(mix.shape[1]), row(D_MODEL), full(w_xq),
                  pl.BlockSpec((mem_len, XA_DIM), lambda b, t: (b, 0)),
                  pl.BlockSpec((mem_len, XA_DIM), lambda b, t: (b, 1)),
                  full(w_o), full(ln1_g), full(ln1_b), full(w1), full(w2), full(ln2_g), full(ln2_b)],
        out_specs=row(D_MODEL),
        scratch_shapes=[pltpu.VMEM((tm, D_MODEL), F32), pltpu.VMEM((tm, D_FF), _MXU_DTYPE)],
        compiler_params=pltpu.CompilerParams(
            dimension_semantics=("parallel", "parallel"), vmem_limit_bytes=_VMEM_LIMIT),
    )(mix, x, w_xq, kv, kv, w_o, ln1_g, ln1_b, w1, w2, ln2_g, ln2_b)


def _mlp_kernel(x_ref, w1_ref, w2_ref, g_ref, b_ref, o_ref, h_ref):
    tm = x_ref.shape[0]
    sub = tm // _MLP_SPLIT
    for s in range(_MLP_SPLIT):
        rs = slice(s * sub, (s + 1) * sub)
        xb = _mxu(x_ref[rs, :])
        for j in range(D_FF // _MLP_TF):
            fs = slice(j * _MLP_TF, (j + 1) * _MLP_TF)
            hid = jnp.dot(xb, w1_ref[:, fs], preferred_element_type=F32)
            h_ref[rs, fs] = _mxu(jnp.square(jnp.maximum(hid, 0.0)))
        y = jnp.dot(h_ref[rs, :], w2_ref[...], preferred_element_type=F32)
        o_ref[rs, :] = _layer_norm(DN_ALPHA * x_ref[rs, :] + y, g_ref[...], b_ref[...])


def _row(v, offset=0):
    width = -(-(offset + v.shape[0]) // LANES) * LANES
    return jnp.zeros((1, width), F32).at[0, offset:offset + v.shape[0]].set(v.astype(F32))


def kernel(x, mem, w_kv_mem, w_o, ln1_g, ln1_b, ln2_g, ln2_b, mlp_w1, mlp_w2, gdn_w_in, gdn_conv_w,
           gdn_a_log, gdn_dt_bias, gdn_norm_g, s5_w_in, s5_a_re, s5_a_im, s5_b_re, s5_b_im, s5_c_re,
           s5_c_im, s5_log_dt, s5_d, s5_w_glu, s5_b_glu):
    bsz, length, _ = x.shape
    tokens = bsz * length
    xf = x.reshape(tokens, D_MODEL)
    memf = mem.reshape(bsz * mem.shape[1], D_MODEL)
    qkvz = 4 * GDN_DIM
    for i in range(DEPTH):
        j = i // 2
        kv = _matmul(memf, w_kv_mem[i].astype(_MXU_DTYPE), tm=memf.shape[0], tn=2 * XA_DIM)
        if i % 2 == 0:
            w_in = gdn_w_in[j]
            w_ba = jnp.pad(w_in[:, qkvz:qkvz + 2 * GDN_HEADS], ((0, 0), (0, LANES - 2 * GDN_HEADS)))
            w_xq = w_in[:, qkvz + 2 * GDN_HEADS:]
            mix = _gdn_mixer(xf, w_in[:, :qkvz].astype(_MXU_DTYPE), w_ba.astype(_MXU_DTYPE),
                             gdn_conv_w[j].astype(F32), _row(gdn_a_log[j], GDN_HEADS),
                             _row(gdn_dt_bias[j], GDN_HEADS), _row(gdn_norm_g[j]), bsz=bsz, length=length)
        else:
            w_xq = s5_w_in[j][:, S5_DIM:]
            abar_re, abar_im, bbar_re, bbar_im = _s5_discretize(
                s5_a_re[j].astype(F32), s5_a_im[j].astype(F32), s5_log_dt[j].astype(F32),
                s5_b_re[j].astype(F32), s5_b_im[j].astype(F32))
            wb, wc, a_tab_re, a_tab_im = _s5_tables(
                abar_re, abar_im, bbar_re, bbar_im, s5_c_re[j].astype(F32), s5_c_im[j].astype(F32), bsz)
            mix = _s5_mixer(xf.reshape(bsz, length, D_MODEL), s5_w_in[j][:, :S5_DIM].astype(_MXU_DTYPE),
                            wb, wc, a_tab_re, a_tab_im, _row(s5_d[j]), s5_w_glu[j].astype(_MXU_DTYPE),
                            _row(s5_b_glu[j]))
            mix = mix.reshape(tokens, S5_DIM)
        xf = _layer_tail(mix, xf, w_xq.astype(_MXU_DTYPE), kv, w_o[i].astype(_MXU_DTYPE), _row(ln1_g[i]),
                         _row(ln1_b[i]), mlp_w1[i].astype(_MXU_DTYPE), mlp_w2[i].astype(_MXU_DTYPE),
                         _row(ln2_g[i]), _row(ln2_b[i]), bsz=bsz, length=length)
    return xf.reshape(bsz, length, D_MODEL)
```

```python
import functools

import jax
import jax.numpy as jnp
from jax import lax
from jax.experimental import pallas as pl
from jax.experimental.pallas import tpu as pltpu

F32 = jnp.float32
_MXU_DTYPE = jnp.bfloat16

D_MODEL = 1024
DEPTH = 4
GDN_HEADS = 8
HEAD_DIM = 128
GDN_DIM = GDN_HEADS * HEAD_DIM
GDN_CONV = 4
GDN_CHUNK = 64
S5_DIM = D_MODEL
S5_GROUP = 16
S5_GROUPS = S5_DIM // S5_GROUP
S5_STATE = 64
XA_HEADS = 4
XA_DIM = XA_HEADS * HEAD_DIM
D_FF = 4 * D_MODEL
DN_ALPHA = (2 * DEPTH) ** 0.25
LN_EPS = 1e-5
RMS_EPS = 1e-6

LANES = 128
SUBLANES = 8
_VMEM_LIMIT = 56 * 1024 * 1024

_GDN_TT = 256
_GDN_G = 2 * GDN_CHUNK
_GDN_HEAD_BATCH = 8
_S5_TT = 128
_S5_UNROLL = 4
_S5_SCAN_PAIRS = 1
_S5_PITCH = _S5_TT + SUBLANES // 2
_S5_CB = 8
_MIX_TM = 512
_MIX_SPLIT = 2
_MLP_SPLIT = 2
_MLP_TF = 1024


def _mxu(a):
    return a.astype(_MXU_DTYPE)


def _dot(a, b):
    return jnp.dot(_mxu(a), _mxu(b), preferred_element_type=F32)


def _dot_nt(a, b):
    return lax.dot_general(_mxu(a), _mxu(b), (((1,), (1,)), ((), ())), preferred_element_type=F32)


def _dot_tn(a, b):
    return lax.dot_general(_mxu(a), _mxu(b), (((0,), (0,)), ((), ())), preferred_element_type=F32)


def _dot_exact_lhs(a01, b):
    hi = b.astype(_MXU_DTYPE)
    r1 = b - hi.astype(F32)
    mid = r1.astype(_MXU_DTYPE)
    lo = (r1 - mid.astype(F32)).astype(_MXU_DTYPE)
    a = a01.astype(_MXU_DTYPE)
    acc = jnp.dot(a, lo, preferred_element_type=F32)
    acc = acc + jnp.dot(a, mid, preferred_element_type=F32)
    return acc + jnp.dot(a, hi, preferred_element_type=F32)


def _layer_norm(y, g, b):
    mu = jnp.mean(y, axis=-1, keepdims=True)
    yc = y - mu
    var = jnp.mean(jnp.square(yc), axis=-1, keepdims=True)
    return yc * lax.rsqrt(var + LN_EPS) * g + b


def _softplus(x):
    return jnp.maximum(x, 0.0) + jnp.log1p(jnp.exp(-jnp.abs(x)))


def _matmul_kernel(x_ref, w_ref, o_ref):
    o_ref[...] = jnp.dot(_mxu(x_ref[...]), w_ref[...], preferred_element_type=F32)


def _matmul(x, w, *, tm, tn):
    m, k = x.shape
    n = w.shape[1]
    return pl.pallas_call(
        _matmul_kernel,
        out_shape=jax.ShapeDtypeStruct((m, n), F32),
        grid=(m // tm, n // tn),
        in_specs=[pl.BlockSpec((tm, k), lambda i, j: (i, 0)),
                  pl.BlockSpec((k, tn), lambda i, j: (0, j))],
        out_specs=pl.BlockSpec((tm, tn), lambda i, j: (i, j)),
        compiler_params=pltpu.CompilerParams(
            dimension_semantics=("parallel", "parallel"), vmem_limit_bytes=_VMEM_LIMIT),
    )(x, w)


def _gdn_kernel(x_ref, w_ref, cw_ref, alog_ref, dtb_ref, ng_ref, o_ref,
                tail_ref, qs_ref, ks_ref, vs_ref, zs_ref, uw_ref, qg_ref, kd_ref, attn_ref,
                el_ref, vn_ref, oi_ref, op_ref, s_ref):
    tt = x_ref.shape[0]
    g_rows = _GDN_G
    c_rows = GDN_CHUNK
    n_groups = tt // g_rows
    cpg = g_rows // c_rows
    vpc = c_rows // SUBLANES
    n_wrap = GDN_CONV - 1

    @pl.when(pl.program_id(1) == 0)
    def _():
        tail_ref[...] = jnp.zeros_like(tail_ref)
        s_ref[...] = jnp.zeros_like(s_ref)

    def time_of(pos):
        local = pos % c_rows
        return (pos // c_rows) * c_rows + local // SUBLANES + SUBLANES * (local % SUBLANES)

    prow = lax.broadcasted_iota(jnp.int32, (tt, tt), 0)
    pcol = lax.broadcasted_iota(jnp.int32, (tt, tt), 1)
    perm = (pcol == time_of(prow)).astype(_MXU_DTYPE)
    xp = jnp.dot(perm, _mxu(x_ref[...]), preferred_element_type=F32).astype(_MXU_DTYPE)

    trow = time_of(lax.broadcasted_iota(jnp.int32, (g_rows, g_rows), 0))
    tcol = time_of(lax.broadcasted_iota(jnp.int32, (g_rows, g_rows), 1))
    same = (trow // c_rows) == (tcol // c_rows)
    causal = same & (trow >= tcol)
    strict = same & (trow > tcol)
    blk16 = (trow // 16) == (tcol // 16)
    blk32 = (trow // 32) == (tcol // 32)
    eye = (trow == tcol).astype(F32)
    ltri = causal.astype(F32)
    lane = lax.broadcasted_iota(jnp.int32, (g_rows, LANES), 1)
    rowi = lax.broadcasted_iota(jnp.int32, (cpg * SUBLANES, GDN_DIM), 0)

    def every(fn, *lists):
        return [fn(*args) for args in zip(*lists)]

    heads = range(GDN_HEADS)
    head_lanes = [slice(hl * HEAD_DIM, (hl + 1) * HEAD_DIM) for hl in heads]
    tails = [[tail_ref[idx, w] for w in range(n_wrap)] for idx in range(3)]
    gates = {}
    projected = {}
    states = [s_ref[hl] for hl in heads]


    def project(gi):
        rs = slice(gi * g_rows, (gi + 1) * g_rows)
        for idx, dst in enumerate((qs_ref, ks_ref, vs_ref)):
            sec = slice(idx * GDN_DIM, (idx + 1) * GDN_DIM)
            if gi == 0:
                projected[idx] = jnp.dot(xp, w_ref[:, sec], preferred_element_type=F32)
            y = projected[idx]
            yield

            def slab(c, j):
                r0 = gi * g_rows + c * c_rows + j * SUBLANES
                return y[r0:r0 + SUBLANES, :]

            shifted = {}
            for w in range(n_wrap):
                v = vpc - n_wrap + w
                wv = jnp.concatenate([slab(c, v) for c in range(cpg)], axis=0)
                carry = tails[idx][w][SUBLANES - 1:SUBLANES, :]
                shifted[v] = jnp.where(rowi == 0, carry, pltpu.roll(wv, 1, axis=0))
                tails[idx][w] = slab(cpg - 1, v)
            taps = [cw_ref[tap:tap + 1, sec] for tap in range(GDN_CONV)]
            for c in range(cpg):
                for j in range(vpc):
                    def src(d):
                        if j >= d:
                            return slab(c, j - d)
                        return shifted[j - d + vpc][c * SUBLANES:(c + 1) * SUBLANES, :]
                    acc = src(3) * taps[0]
                    for tap in range(1, GDN_CONV):
                        acc = acc + src(GDN_CONV - 1 - tap) * taps[tap]
                    r0 = gi * g_rows + c * c_rows + j * SUBLANES
                    dst[r0:r0 + SUBLANES, :] = acc * jax.nn.sigmoid(acc)
        if gi == 0:
            zs_ref[...] = jnp.dot(xp, w_ref[:, 3 * GDN_DIM:4 * GDN_DIM], preferred_element_type=F32)
            projected[3] = jnp.dot(xp, w_ref[:, 4 * GDN_DIM:4 * GDN_DIM + LANES],
                                   preferred_element_type=F32)
        yield
        ba = projected[3][rs]
        gates[gi] = (jax.nn.sigmoid(ba), -jnp.exp(alog_ref[...]) * _softplus(ba + dtb_ref[...]))
        yield

    def prepare(gi):
        beta_all, g_all = gates.pop(gi)
        gc_all = _dot_exact_lhs(ltri, g_all)
        yield
        for h0 in range(0, GDN_HEADS, _GDN_HEAD_BATCH):
            yield from prepare_heads(gi, range(h0, h0 + _GDN_HEAD_BATCH), beta_all, gc_all)

    def prepare_heads(gi, heads, beta_all, gc_all):
        rs = slice(gi * g_rows, (gi + 1) * g_rows)
        head_lanes = [slice(hl * HEAD_DIM, (hl + 1) * HEAD_DIM) for hl in heads]

        def l2n(ref, hs):
            t = ref[rs, hs]
            return t * lax.rsqrt(jnp.sum(jnp.square(t), axis=-1, keepdims=True) + 1e-6)

        def gate_col(t, pos):
            col = jnp.sum(jnp.where(lane == pos, t, 0.0), axis=-1, keepdims=True)
            return jnp.broadcast_to(col, (g_rows, HEAD_DIM))

        q = [l2n(qs_ref, hs) * (HEAD_DIM ** -0.5) for hs in head_lanes]
        k = [l2n(ks_ref, hs) for hs in head_lanes]
        bet = [gate_col(beta_all, hl) for hl in heads]
        gcb = [gate_col(gc_all, hl + GDN_HEADS) for hl in heads]
        decay = every(lambda t: jnp.where(causal, jnp.exp(jnp.where(causal, t - t.T, 0.0)), 0.0), gcb)
        kb = every(lambda a, b: a * b, k, bet)
        kk = every(_dot_nt, kb, k)
        yield
        qk = every(_dot_nt, q, k)
        yield
        a_mat = every(lambda m, d: jnp.where(strict, m * d, 0.0), kk, decay)
        for hl, m, d in zip(heads, qk, decay):
            attn_ref[hl, rs, :] = m * d
        d1 = every(lambda a: jnp.where(blk16, a, 0.0), a_mat)
        d2 = every(_dot, d1, d1)
        yield
        d4 = every(_dot, d2, d2)
        yield
        d8 = every(_dot, d4, d4)
        yield
        t_inv = every(lambda d: eye - d, d1)
        for dn in (d2, d4, d8):
            t_inv = every(lambda t, p: t + p, t_inv, every(_dot, t_inv, dn))
            yield
        for off_diag in (lambda a: jnp.where(blk32 & jnp.logical_not(blk16), a, 0.0),
                         lambda a: jnp.where(blk32, 0.0, a)):
            te = every(_dot, t_inv, every(off_diag, a_mat))
            yield
            t_inv = every(lambda t, p: t - p, t_inv, every(_dot, te, t_inv))
            yield
        egc = every(jnp.exp, gcb)
        rhs = [jnp.concatenate([vs_ref[rs, hs] * b, kbi * e], axis=1)
               for hs, b, kbi, e in zip(head_lanes, bet, kb, egc)]
        uw = every(_dot, t_inv, rhs)
        yield
        for hl, uwi, qi, ki, e, g in zip(heads, uw, q, k, egc, gcb):
            uw_ref[hl, rs, :] = uwi
            qg_ref[hl, rs, :] = qi * e
            for c in range(cpg):
                lo = c * c_rows
                hi = lo + c_rows
                g_last = g[hi - 1:hi, :]
                kd_ref[hl, gi * g_rows + lo:gi * g_rows + hi, :] = ki[lo:hi] * jnp.exp(g_last - g[lo:hi])
                ci = gi * cpg + c
                el_ref[hl, ci * SUBLANES:(ci + 1) * SUBLANES, :] = jnp.broadcast_to(
                    jnp.exp(g_last), (SUBLANES, HEAD_DIM))

    def recur(gi):
        for ci in range(gi * cpg, (gi + 1) * cpg):
            cs = slice(ci * c_rows, (ci + 1) * c_rows)
            lhs = [jnp.concatenate([uw_ref[hl, cs, HEAD_DIM:], qg_ref[hl, cs, :]], axis=0) for hl in heads]
            res = every(_dot, lhs, states)
            yield
            vn = [uw_ref[hl, cs, :HEAD_DIM] - res[hl][:c_rows] for hl in heads]
            upd = [_dot_tn(kd_ref[hl, cs, :], vn[hl]) for hl in heads]
            yield
            for hl in heads:
                vn_ref[hl, cs, :] = vn[hl]
                oi_ref[hl, cs, :] = res[hl][c_rows:]
                states[hl] = states[hl] * el_ref[hl, ci * SUBLANES:ci * SUBLANES + 1, :] + upd[hl]

    def finish(gi):
        rs = slice(gi * g_rows, (gi + 1) * g_rows)
        intra = [_dot(attn_ref[hl, rs, :], vn_ref[hl, rs, :]) for hl in heads]
        yield
        for hl, hs in zip(heads, head_lanes):
            o = oi_ref[hl, rs, :] + intra[hl]
            o = o * lax.rsqrt(jnp.mean(jnp.square(o), axis=-1, keepdims=True) + RMS_EPS) * ng_ref[...]
            z = zs_ref[rs, hs]
            op_ref[rs, hs] = (o * (z * jax.nn.sigmoid(z))).astype(op_ref.dtype)

    stage_counts = (5, 1 + 13 * (GDN_HEADS // _GDN_HEAD_BATCH), 2 * cpg, 1)

    def interleave(pipes):
        done = [0] * len(pipes)
        live = set(range(len(pipes)))
        while live:
            i = min(live, key=lambda p: (done[p] + 1) / (pipes[p][1] + 1))
            try:
                next(pipes[i][0])
                done[i] += 1
            except StopIteration:
                live.remove(i)

    stages = (project, prepare, recur, finish)
    for rnd in range(n_groups + len(stages) - 1):
        pipes = [(stage(rnd - lag), stage_counts[lag]) for lag, stage in enumerate(stages)
                 if 0 <= rnd - lag < n_groups]
        interleave(pipes)

    for idx in range(3):
        for w in range(n_wrap):
            tail_ref[idx, w] = tails[idx][w]
    for hl in heads:
        s_ref[hl] = states[hl]
    o_ref[...] = jnp.dot(perm, op_ref[...], preferred_element_type=F32).astype(o_ref.dtype)


def _gdn_mixer(x, w_in_all, layer, conv_w, alog_row, dtb_row, norm_g, *, bsz, length):
    tt = min(_GDN_TT, length)
    nt = length // tt
    full = lambda a: pl.BlockSpec(a.shape, lambda b, t: (0,) * a.ndim)
    per_head = lambda rows, cols: pltpu.VMEM((GDN_HEADS, rows, cols), F32)
    tile = lambda: pltpu.VMEM((tt, GDN_DIM), F32)
    return pl.pallas_call(
        _gdn_kernel,
        out_shape=jax.ShapeDtypeStruct((bsz * length, GDN_DIM), _MXU_DTYPE),
        grid=(bsz, nt),
        in_specs=[pl.BlockSpec((tt, D_MODEL), lambda b, t: (b * nt + t, 0)),
                  pl.BlockSpec((None,) + w_in_all.shape[1:], lambda b, t: (layer, 0, 0)),
                  full(conv_w), full(alog_row), full(dtb_row), full(norm_g)],
        out_specs=pl.BlockSpec((tt, GDN_DIM), lambda b, t: (b * nt + t, 0)),
        scratch_shapes=[pltpu.VMEM((3, GDN_CONV - 1, SUBLANES, GDN_DIM), F32),
                        tile(), tile(), tile(), tile(),
                        per_head(tt, 2 * HEAD_DIM), per_head(tt, HEAD_DIM), per_head(tt, HEAD_DIM),
                        per_head(tt, _GDN_G), per_head((tt // GDN_CHUNK) * SUBLANES, HEAD_DIM),
                        per_head(tt, HEAD_DIM), per_head(tt, HEAD_DIM),
                        pltpu.VMEM((tt, GDN_DIM), _MXU_DTYPE),
                        per_head(HEAD_DIM, HEAD_DIM)],
        compiler_params=pltpu.CompilerParams(
            dimension_semantics=("parallel", "arbitrary"), vmem_limit_bytes=_VMEM_LIMIT),
    )(x, w_in_all, conv_w, alog_row, dtb_row, norm_g)


def _s5_discretize_kernel(are_ref, aim_ref, ldt_ref, bre_ref, bim_ref,
                          abre_ref, abim_ref, bbre_ref, bbim_ref):
    a_re = are_ref[...]
    a_im = aim_ref[...]
    dt = jnp.exp(ldt_ref[...])
    mag = jnp.exp(a_re * dt)
    abar_re = mag * jnp.cos(a_im * dt)
    abar_im = mag * jnp.sin(a_im * dt)
    den = jnp.square(a_re) + jnp.square(a_im)
    n_re = abar_re - 1.0
    n_im = abar_im
    f_re = (n_re * a_re + n_im * a_im) / den
    f_im = (n_im * a_re - n_re * a_im) / den
    abre_ref[...] = abar_re
    abim_ref[...] = abar_im
    bbre_ref[...] = f_re * bre_ref[...] - f_im * bim_ref[...]
    bbim_ref[...] = f_re * bim_ref[...] + f_im * bre_ref[...]


def _s5_discretize(a_re, a_im, log_dt, b_re, b_im):
    rows = S5_GROUPS * S5_GROUP
    rep = lambda t: jnp.repeat(t, S5_GROUP, axis=0)
    b2 = lambda t: jnp.transpose(t, (0, 2, 1)).reshape(rows, S5_STATE)
    ldt = jnp.broadcast_to(log_dt[:, None], (S5_GROUPS, S5_STATE))
    shp = jax.ShapeDtypeStruct((rows, S5_STATE), F32)
    return pl.pallas_call(_s5_discretize_kernel, out_shape=(shp, shp, shp, shp))(
        rep(a_re), rep(a_im), rep(ldt), b2(b_re), b2(b_im))


def _s5_kernel(x_ref, wu_ref, wb_ref, wc_ref, are_ref, aim_ref, d_ref, wg_ref, bg_ref, o_ref,
               us_ref, hs_ref, hre_ref, him_ref):
    nb, tt, _ = x_ref.shape
    pitch = _S5_PITCH
    n_lt = S5_DIM // LANES
    half = n_lt // 2

    @pl.when(pl.program_id(0) == 0)
    def _():
        hre_ref[...] = jnp.zeros_like(hre_ref)
        him_ref[...] = jnp.zeros_like(him_ref)

    us_ref[...] = jnp.dot(_mxu(x_ref[...].reshape(nb * tt, D_MODEL)), wu_ref[...],
                          preferred_element_type=F32)
    n_cp = _S5_CB // 2

    def drive(cp):
        for c in (2 * cp, 2 * cp + 1):
            bu = jnp.dot(_mxu(us_ref[:, c * LANES:(c + 1) * LANES]), wb_ref[c], preferred_element_type=F32)
            for b in range(nb):
                s = (c % 2) * nb + b
                for lt in range(n_lt):
                    hs_ref[cp * n_lt + lt, s * pitch:s * pitch + tt, :] = (
                        bu[b * tt:(b + 1) * tt, lt * LANES:(lt + 1) * LANES])

    def scan(cps):
        tiles = [(cp, l) for cp in cps for l in range(half)]
        a_re = [are_ref[cp * half + l] for cp, l in tiles]
        a_im = [aim_ref[cp * half + l] for cp, l in tiles]

        def step(t, carry):
            out = []
            rows = pl.ds(t, SUBLANES, stride=pitch)
            for i, (cp, l) in enumerate(tiles):
                h_re, h_im = carry[2 * i], carry[2 * i + 1]
                sl_re = cp * n_lt + l
                sl_im = cp * n_lt + half + l
                n_re = a_re[i] * h_re - a_im[i] * h_im + hs_ref[sl_re, rows, :]
                n_im = a_re[i] * h_im + a_im[i] * h_re + hs_ref[sl_im, rows, :]
                hs_ref[sl_re, rows, :] = n_re
                hs_ref[sl_im, rows, :] = n_im
                out += [n_re, n_im]
            return tuple(out)

        init = []
        for cp, l in tiles:
            init += [hre_ref[cp * half + l], him_ref[cp * half + l]]
        fin = lax.fori_loop(0, tt, step, tuple(init), unroll=_S5_UNROLL)
        for i, (cp, l) in enumerate(tiles):
            hre_ref[cp * half + l] = fin[2 * i]
            him_ref[cp * half + l] = fin[2 * i + 1]

    def readout(cp):
        out = {}
        for c in (2 * cp, 2 * cp + 1):
            for b in range(nb):
                s = (c % 2) * nb + b
                hcat = jnp.concatenate(
                    [hs_ref[cp * n_lt + lt, s * pitch:s * pitch + tt, :] for lt in range(n_lt)], axis=1)
                out[(b, c)] = jnp.dot(_mxu(hcat), wc_ref[c], preferred_element_type=F32)
        return out

    ys = {}
    for cp in range(n_cp):
        drive(cp)
    for c0 in range(0, n_cp, _S5_SCAN_PAIRS):
        scan(range(c0, c0 + _S5_SCAN_PAIRS))
    for cp in range(n_cp):
        ys.update(readout(cp))

    for b in range(nb):
        y = jnp.concatenate([ys[(b, c)] for c in range(_S5_CB)], axis=1)
        y = y + d_ref[...] * us_ref[b * tt:(b + 1) * tt, :]
        zg = jax.nn.gelu(y)
        gate = jnp.dot(_mxu(zg), wg_ref[...], preferred_element_type=F32) + bg_ref[...]
        o_ref[b] = (zg * jax.nn.sigmoid(gate)).astype(o_ref.dtype)


def _s5_mixer(x, w_u, wb, wc, a_tab_re, a_tab_im, d_skip, w_glu, b_glu):
    bsz, length, _ = x.shape
    tt = _S5_TT
    n_slab = (_S5_CB // 2) * (S5_DIM // LANES)
    n_pair = n_slab // 2
    full = lambda shape: pl.BlockSpec(shape, lambda t: (0,) * len(shape))
    return pl.pallas_call(
        _s5_kernel,
        out_shape=jax.ShapeDtypeStruct((bsz, length, S5_DIM), _MXU_DTYPE),
        grid=(length // tt,),
        in_specs=[pl.BlockSpec((bsz, tt, D_MODEL), lambda t: (0, t, 0)),
                  full(w_u.shape), full(wb.shape), full(wc.shape), full(a_tab_re.shape), full(a_tab_im.shape),
                  full(d_skip.shape), full(w_glu.shape), full(b_glu.shape)],
        out_specs=pl.BlockSpec((bsz, tt, S5_DIM), lambda t: (0, t, 0)),
        scratch_shapes=[pltpu.VMEM((bsz * tt, S5_DIM), F32),
                        pltpu.VMEM((n_slab, SUBLANES * _S5_PITCH, LANES), F32),
                        pltpu.VMEM((n_pair, SUBLANES, LANES), F32),
                        pltpu.VMEM((n_pair, SUBLANES, LANES), F32)],
        compiler_params=pltpu.CompilerParams(
            dimension_semantics=("arbitrary",), vmem_limit_bytes=_VMEM_LIMIT),
    )(x, w_u, wb, wc, a_tab_re, a_tab_im, d_skip, w_glu, b_glu)


def _s5_tables(abar_re, abar_im, bbar_re, bbar_im, c_re, c_im, bsz):
    gpb = S5_GROUPS // _S5_CB
    eye = jnp.eye(gpb, dtype=F32)

    def drive(bb):
        t = bb.reshape(_S5_CB, gpb, S5_GROUP, S5_STATE)
        t = t[:, :, :, None, :] * eye[None, :, None, :, None]
        return t.reshape(_S5_CB, gpb * S5_GROUP, gpb * S5_STATE)

    def readout(cc):
        t = cc.reshape(_S5_CB, gpb, S5_GROUP, S5_STATE)
        t = jnp.transpose(t, (0, 3, 1, 2))[:, None, :, :, :] * eye[None, :, None, :, None]
        return t.reshape(_S5_CB, gpb * S5_STATE, gpb * S5_GROUP)

    wb = jnp.concatenate([drive(bbar_re), drive(bbar_im)], axis=2).astype(_MXU_DTYPE)
    wc = jnp.concatenate([readout(c_re), -readout(c_im)], axis=1).astype(_MXU_DTYPE)

    def a_tab(ab):
        flat = ab[::S5_GROUP].reshape(_S5_CB // 2, 2, 1, (gpb * S5_STATE) // LANES, LANES)
        t = jnp.broadcast_to(flat, (_S5_CB // 2, 2, bsz, (gpb * S5_STATE) // LANES, LANES))
        t = jnp.transpose(t, (0, 3, 1, 2, 4))
        return t.reshape((_S5_CB // 2) * ((gpb * S5_STATE) // LANES), 2 * bsz, LANES)

    return wb, wc, a_tab(abar_re), a_tab(abar_im)


def _mixout_kernel(mix_ref, x_ref, wq_ref, k_ref, v_ref, wo_ref, g_ref, b_ref, o_ref):
    tm = x_ref.shape[0]
    sub = tm // _MIX_SPLIT
    parts = [slice(s * sub, (s + 1) * sub) for s in range(_MIX_SPLIT)]
    head_lanes = [slice(hh * HEAD_DIM, (hh + 1) * HEAD_DIM) for hh in range(XA_HEADS)]
    mix_dim = mix_ref.shape[1]
    xq = [jnp.dot(_mxu(x_ref[rs, :]), wq_ref[...], preferred_element_type=F32) for rs in parts]
    scores = [[_dot_nt(q[:, sl], k_ref[:, sl]) * (HEAD_DIM ** -0.5) for sl in head_lanes] for q in xq]
    h_mix = [jnp.dot(_mxu(mix_ref[rs, :]), wo_ref[0:mix_dim, :], preferred_element_type=F32) for rs in parts]
    cross = []
    for part_scores in scores:
        outs = []
        for s, sl in zip(part_scores, head_lanes):
            e = jnp.exp(s - jnp.max(s, axis=-1, keepdims=True))
            p = e / jnp.sum(e, axis=-1, keepdims=True)
            outs.append(_dot(p, v_ref[:, sl]))
        cross.append(jnp.concatenate(outs, axis=1))
    for rs, hm, cr in zip(parts, h_mix, cross):
        h = hm + jnp.dot(_mxu(cr), wo_ref[mix_dim:, :], preferred_element_type=F32)
        o_ref[rs, :] = _layer_norm(DN_ALPHA * x_ref[rs, :] + h, g_ref[...], b_ref[...])


def _tail_kernel(mix_ref, x_ref, wq_ref, k_ref, v_ref, wo_ref, g1_ref, b1_ref, w1_ref, w2_ref, g2_ref,
                 b2_ref, o_ref, x1_ref, h_ref):
    _mixout_kernel(mix_ref, x_ref, wq_ref, k_ref, v_ref, wo_ref, g1_ref, b1_ref, x1_ref)
    _mlp_kernel(x1_ref, w1_ref, w2_ref, g2_ref, b2_ref, o_ref, h_ref)


def _layer_tail(mix, x, w_xq, kv, w_o, ln1_g, ln1_b, w1, w2, ln2_g, ln2_b, *, layer, bsz, length):
    tm = min(_MIX_TM, length)
    nt = length // tm
    mem_len = kv.shape[0] // bsz
    row = lambda width: pl.BlockSpec((tm, width), lambda b, t: (b * nt + t, 0))
    full = lambda a: pl.BlockSpec(a.shape, lambda b, t: (0,) * a.ndim, pipeline_mode=pl.Buffered(1))
    of_layer = lambda a: pl.BlockSpec((None,) + a.shape[1:], lambda b, t: (layer, 0, 0),
                                     pipeline_mode=pl.Buffered(1))
    return pl.pallas_call(
        _tail_kernel,
        out_shape=jax.ShapeDtypeStruct(x.shape, F32),
        grid=(bsz, nt),
        in_specs=[row(mix.shape[1]), row(D_MODEL), full(w_xq),
                  pl.BlockSpec((mem_len, XA_DIM), lambda b, t: (b, 0)),
                  pl.BlockSpec((mem_len, XA_DIM), lambda b, t: (b, 1)),
                  of_layer(w_o), full(ln1_g), full(ln1_b), of_layer(w1), of_layer(w2), full(ln2_g), full(ln2_b)],
        out_specs=row(D_MODEL),
        scratch_shapes=[pltpu.VMEM((tm, D_MODEL), F32), pltpu.VMEM((tm, D_FF), _MXU_DTYPE)],
        compiler_params=pltpu.CompilerParams(
            dimension_semantics=("parallel", "parallel"), vmem_limit_bytes=_VMEM_LIMIT),
    )(mix, x, w_xq, kv, kv, w_o, ln1_g, ln1_b, w1, w2, ln2_g, ln2_b)


def _mlp_kernel(x_ref, w1_ref, w2_ref, g_ref, b_ref, o_ref, h_ref):
    tm = x_ref.shape[0]
    sub = tm // _MLP_SPLIT
    for s in range(_MLP_SPLIT):
        rs = slice(s * sub, (s + 1) * sub)
        xb = _mxu(x_ref[rs, :])
        for j in range(D_FF // _MLP_TF):
            fs = slice(j * _MLP_TF, (j + 1) * _MLP_TF)
            hid = jnp.dot(xb, w1_ref[:, fs], preferred_element_type=F32)
            h_ref[rs, fs] = _mxu(jnp.square(jnp.maximum(hid, 0.0)))
        y = jnp.dot(h_ref[rs, :], w2_ref[...], preferred_element_type=F32)
        o_ref[rs, :] = _layer_norm(DN_ALPHA * x_ref[rs, :] + y, g_ref[...], b_ref[...])


def _row(v, offset=0):
    width = -(-(offset + v.shape[0]) // LANES) * LANES
    return jnp.zeros((1, width), F32).at[0, offset:offset + v.shape[0]].set(v.astype(F32))


def kernel(x, mem, w_kv_mem, w_o, ln1_g, ln1_b, ln2_g, ln2_b, mlp_w1, mlp_w2, gdn_w_in, gdn_conv_w,
           gdn_a_log, gdn_dt_bias, gdn_norm_g, s5_w_in, s5_a_re, s5_a_im, s5_b_re, s5_b_im, s5_c_re,
           s5_c_im, s5_log_dt, s5_d, s5_w_glu, s5_b_glu):
    bsz, length, _ = x.shape
    tokens = bsz * length
    xf = x.reshape(tokens, D_MODEL)
    memf = mem.reshape(bsz * mem.shape[1], D_MODEL)
    qkvz = 4 * GDN_DIM
    gdn_w_all = gdn_w_in.astype(_MXU_DTYPE)
    w1_all = mlp_w1.astype(_MXU_DTYPE)
    w2_all = mlp_w2.astype(_MXU_DTYPE)
    wo_all = w_o.astype(_MXU_DTYPE)
    for i in range(DEPTH):
        j = i // 2
        kv = _matmul(memf, w_kv_mem[i].astype(_MXU_DTYPE), tm=memf.shape[0], tn=2 * XA_DIM)
        if i % 2 == 0:
            w_xq = gdn_w_in[j][:, qkvz + 2 * GDN_HEADS:]
            mix = _gdn_mixer(xf, gdn_w_all, j, gdn_conv_w[j].astype(F32), _row(gdn_a_log[j], GDN_HEADS),
                             _row(gdn_dt_bias[j], GDN_HEADS), _row(gdn_norm_g[j]), bsz=bsz, length=length)
        else:
            w_xq = s5_w_in[j][:, S5_DIM:]
            abar_re, abar_im, bbar_re, bbar_im = _s5_discretize(
                s5_a_re[j].astype(F32), s5_a_im[j].astype(F32), s5_log_dt[j].astype(F32),
                s5_b_re[j].astype(F32), s5_b_im[j].astype(F32))
            wb, wc, a_tab_re, a_tab_im = _s5_tables(
                abar_re, abar_im, bbar_re, bbar_im, s5_c_re[j].astype(F32), s5_c_im[j].astype(F32), bsz)
            mix = _s5_mixer(xf.reshape(bsz, length, D_MODEL), s5_w_in[j][:, :S5_DIM].astype(_MXU_DTYPE),
                            wb, wc, a_tab_re, a_tab_im, _row(s5_d[j]), s5_w_glu[j].astype(_MXU_DTYPE),
                            _row(s5_b_glu[j]))
            mix = mix.reshape(tokens, S5_DIM)
        xf = _layer_tail(mix, xf, w_xq.astype(_MXU_DTYPE), kv, wo_all, _row(ln1_g[i]), _row(ln1_b[i]),
                         w1_all, w2_all, _row(ln2_g[i]), _row(ln2_b[i]), layer=i, bsz=bsz, length=length)
    return xf.reshape(bsz, length, D_MODEL)
```

```python
import functools

import jax
import jax.numpy as jnp
from jax import lax
from jax.experimental import pallas as pl
from jax.experimental.pallas import tpu as pltpu

F32 = jnp.float32
_MXU_DTYPE = jnp.bfloat16

D_MODEL = 1024
DEPTH = 4
GDN_HEADS = 8
HEAD_DIM = 128
GDN_DIM = GDN_HEADS * HEAD_DIM
GDN_CONV = 4
GDN_CHUNK = 64
S5_DIM = D_MODEL
S5_GROUP = 16
S5_GROUPS = S5_DIM // S5_GROUP
S5_STATE = 64
XA_HEADS = 4
XA_DIM = XA_HEADS * HEAD_DIM
D_FF = 4 * D_MODEL
DN_ALPHA = (2 * DEPTH) ** 0.25
LN_EPS = 1e-5
RMS_EPS = 1e-6

LANES = 128
SUBLANES = 8
_VMEM_LIMIT = 56 * 1024 * 1024

_GDN_TT = 256
_GDN_G = 2 * GDN_CHUNK
_GDN_HEAD_BATCH = 8
_S5_TT = 128
_S5_UNROLL = 8
_S5_SCAN_PAIRS = 1
_S5_PITCH = _S5_TT + SUBLANES // 2
_S5_CB = 8
_MIX_TM = 512
_MIX_SPLIT = 2
_MLP_SPLIT = 2
_MLP_TF = 1024


def _mxu(a):
    return a.astype(_MXU_DTYPE)


def _dot(a, b):
    return jnp.dot(_mxu(a), _mxu(b), preferred_element_type=F32)


def _dot_nt(a, b):
    return lax.dot_general(_mxu(a), _mxu(b), (((1,), (1,)), ((), ())), preferred_element_type=F32)


def _dot_tn(a, b):
    return lax.dot_general(_mxu(a), _mxu(b), (((0,), (0,)), ((), ())), preferred_element_type=F32)


def _dot_exact_lhs(a01, b):
    hi = b.astype(_MXU_DTYPE)
    r1 = b - hi.astype(F32)
    mid = r1.astype(_MXU_DTYPE)
    lo = (r1 - mid.astype(F32)).astype(_MXU_DTYPE)
    a = a01.astype(_MXU_DTYPE)
    acc = jnp.dot(a, lo, preferred_element_type=F32)
    acc = acc + jnp.dot(a, mid, preferred_element_type=F32)
    return acc + jnp.dot(a, hi, preferred_element_type=F32)


def _layer_norm(y, g, b):
    mu = jnp.mean(y, axis=-1, keepdims=True)
    yc = y - mu
    var = jnp.mean(jnp.square(yc), axis=-1, keepdims=True)
    return yc * lax.rsqrt(var + LN_EPS) * g + b


def _softplus(x):
    return jnp.maximum(x, 0.0) + jnp.log1p(jnp.exp(-jnp.abs(x)))


def _matmul_kernel(x_ref, w_ref, o_ref):
    o_ref[...] = jnp.dot(_mxu(x_ref[...]), w_ref[...], preferred_element_type=F32)


def _matmul(x, w, *, tm, tn):
    m, k = x.shape
    n = w.shape[1]
    return pl.pallas_call(
        _matmul_kernel,
        out_shape=jax.ShapeDtypeStruct((m, n), F32),
        grid=(m // tm, n // tn),
        in_specs=[pl.BlockSpec((tm, k), lambda i, j: (i, 0)),
                  pl.BlockSpec((k, tn), lambda i, j: (0, j))],
        out_specs=pl.BlockSpec((tm, tn), lambda i, j: (i, j)),
        compiler_params=pltpu.CompilerParams(
            dimension_semantics=("parallel", "parallel"), vmem_limit_bytes=_VMEM_LIMIT),
    )(x, w)


def _gdn_kernel(x_ref, w_ref, cw_ref, alog_ref, dtb_ref, ng_ref, o_ref,
                tail_ref, qs_ref, ks_ref, vs_ref, zs_ref, uw_ref, qg_ref, kd_ref, attn_ref,
                el_ref, vn_ref, oi_ref, op_ref, s_ref):
    tt = x_ref.shape[0]
    g_rows = _GDN_G
    c_rows = GDN_CHUNK
    n_groups = tt // g_rows
    cpg = g_rows // c_rows
    vpc = c_rows // SUBLANES
    n_wrap = GDN_CONV - 1

    @pl.when(pl.program_id(1) == 0)
    def _():
        tail_ref[...] = jnp.zeros_like(tail_ref)
        s_ref[...] = jnp.zeros_like(s_ref)

    def time_of(pos):
        local = pos % c_rows
        return (pos // c_rows) * c_rows + local // SUBLANES + SUBLANES * (local % SUBLANES)

    prow = lax.broadcasted_iota(jnp.int32, (tt, tt), 0)
    pcol = lax.broadcasted_iota(jnp.int32, (tt, tt), 1)
    perm = (pcol == time_of(prow)).astype(_MXU_DTYPE)
    xp = jnp.dot(perm, _mxu(x_ref[...]), preferred_element_type=F32).astype(_MXU_DTYPE)

    trow = time_of(lax.broadcasted_iota(jnp.int32, (g_rows, g_rows), 0))
    tcol = time_of(lax.broadcasted_iota(jnp.int32, (g_rows, g_rows), 1))
    same = (trow // c_rows) == (tcol // c_rows)
    causal = same & (trow >= tcol)
    strict = same & (trow > tcol)
    blk16 = (trow // 16) == (tcol // 16)
    blk32 = (trow // 32) == (tcol // 32)
    eye = (trow == tcol).astype(F32)
    ltri = causal.astype(F32)
    lane = lax.broadcasted_iota(jnp.int32, (g_rows, LANES), 1)
    rowi = lax.broadcasted_iota(jnp.int32, (cpg * SUBLANES, GDN_DIM), 0)

    def every(fn, *lists):
        return [fn(*args) for args in zip(*lists)]

    heads = range(GDN_HEADS)
    head_lanes = [slice(hl * HEAD_DIM, (hl + 1) * HEAD_DIM) for hl in heads]
    tails = [[tail_ref[idx, w] for w in range(n_wrap)] for idx in range(3)]
    gates = {}
    projected = {}
    states = [s_ref[hl] for hl in heads]


    def project(gi):
        rs = slice(gi * g_rows, (gi + 1) * g_rows)
        for idx, dst in enumerate((qs_ref, ks_ref, vs_ref)):
            sec = slice(idx * GDN_DIM, (idx + 1) * GDN_DIM)
            if gi == 0:
                projected[idx] = jnp.dot(xp, w_ref[:, sec], preferred_element_type=F32)
            y = projected[idx]
            yield

            def slab(c, j):
                r0 = gi * g_rows + c * c_rows + j * SUBLANES
                return y[r0:r0 + SUBLANES, :]

            shifted = {}
            for w in range(n_wrap):
                v = vpc - n_wrap + w
                wv = jnp.concatenate([slab(c, v) for c in range(cpg)], axis=0)
                carry = tails[idx][w][SUBLANES - 1:SUBLANES, :]
                shifted[v] = jnp.where(rowi == 0, carry, pltpu.roll(wv, 1, axis=0))
                tails[idx][w] = slab(cpg - 1, v)
            taps = [cw_ref[tap:tap + 1, sec] for tap in range(GDN_CONV)]
            for c in range(cpg):
                for j in range(vpc):
                    def src(d):
                        if j >= d:
                            return slab(c, j - d)
                        return shifted[j - d + vpc][c * SUBLANES:(c + 1) * SUBLANES, :]
                    acc = src(3) * taps[0]
                    for tap in range(1, GDN_CONV):
                        acc = acc + src(GDN_CONV - 1 - tap) * taps[tap]
                    r0 = gi * g_rows + c * c_rows + j * SUBLANES
                    dst[r0:r0 + SUBLANES, :] = acc * jax.nn.sigmoid(acc)
        if gi == 0:
            zs_ref[...] = jnp.dot(xp, w_ref[:, 3 * GDN_DIM:4 * GDN_DIM], preferred_element_type=F32)
            projected[3] = jnp.dot(xp, w_ref[:, 4 * GDN_DIM:4 * GDN_DIM + LANES],
                                   preferred_element_type=F32)
        yield
        ba = projected[3][rs]
        gates[gi] = (jax.nn.sigmoid(ba), -jnp.exp(alog_ref[...]) * _softplus(ba + dtb_ref[...]))
        yield

    def prepare(gi):
        beta_all, g_all = gates.pop(gi)
        gc_all = _dot_exact_lhs(ltri, g_all)
        yield
        for h0 in range(0, GDN_HEADS, _GDN_HEAD_BATCH):
            yield from prepare_heads(gi, range(h0, h0 + _GDN_HEAD_BATCH), beta_all, gc_all)

    def prepare_heads(gi, heads, beta_all, gc_all):
        rs = slice(gi * g_rows, (gi + 1) * g_rows)
        head_lanes = [slice(hl * HEAD_DIM, (hl + 1) * HEAD_DIM) for hl in heads]

        def l2n(ref, hs):
            t = ref[rs, hs]
            return t * lax.rsqrt(jnp.sum(jnp.square(t), axis=-1, keepdims=True) + 1e-6)

        def gate_col(t, pos):
            col = jnp.sum(jnp.where(lane == pos, t, 0.0), axis=-1, keepdims=True)
            return jnp.broadcast_to(col, (g_rows, HEAD_DIM))

        q = [l2n(qs_ref, hs) * (HEAD_DIM ** -0.5) for hs in head_lanes]
        k = [l2n(ks_ref, hs) for hs in head_lanes]
        bet = [gate_col(beta_all, hl) for hl in heads]
        gcb = [gate_col(gc_all, hl + GDN_HEADS) for hl in heads]
        decay = every(lambda t: jnp.where(causal, jnp.exp(jnp.where(causal, t - t.T, 0.0)), 0.0), gcb)
        kb = every(lambda a, b: a * b, k, bet)
        kk = every(_dot_nt, kb, k)
        yield
        qk = every(_dot_nt, q, k)
        yield
        a_mat = every(lambda m, d: jnp.where(strict, m * d, 0.0), kk, decay)
        for hl, m, d in zip(heads, qk, decay):
            attn_ref[hl, rs, :] = m * d
        d1 = every(lambda a: jnp.where(blk16, a, 0.0), a_mat)
        d2 = every(_dot, d1, d1)
        yield
        d4 = every(_dot, d2, d2)
        yield
        d8 = every(_dot, d4, d4)
        yield
        t_inv = every(lambda d: eye - d, d1)
        for dn in (d2, d4, d8):
            t_inv = every(lambda t, p: t + p, t_inv, every(_dot, t_inv, dn))
            yield
        for off_diag in (lambda a: jnp.where(blk32 & jnp.logical_not(blk16), a, 0.0),
                         lambda a: jnp.where(blk32, 0.0, a)):
            te = every(_dot, t_inv, every(off_diag, a_mat))
            yield
            t_inv = every(lambda t, p: t - p, t_inv, every(_dot, te, t_inv))
            yield
        egc = every(jnp.exp, gcb)
        rhs = [jnp.concatenate([vs_ref[rs, hs] * b, kbi * e], axis=1)
               for hs, b, kbi, e in zip(head_lanes, bet, kb, egc)]
        uw = every(_dot, t_inv, rhs)
        yield
        for hl, uwi, qi, ki, e, g in zip(heads, uw, q, k, egc, gcb):
            uw_ref[hl, rs, :] = uwi
            qg_ref[hl, rs, :] = qi * e
            for c in range(cpg):
                lo = c * c_rows
                hi = lo + c_rows
                g_last = g[hi - 1:hi, :]
                kd_ref[hl, gi * g_rows + lo:gi * g_rows + hi, :] = ki[lo:hi] * jnp.exp(g_last - g[lo:hi])
                ci = gi * cpg + c
                el_ref[hl, ci * SUBLANES:(ci + 1) * SUBLANES, :] = jnp.broadcast_to(
                    jnp.exp(g_last), (SUBLANES, HEAD_DIM))

    def recur(gi):
        for ci in range(gi * cpg, (gi + 1) * cpg):
            cs = slice(ci * c_rows, (ci + 1) * c_rows)
            lhs = [jnp.concatenate([uw_ref[hl, cs, HEAD_DIM:], qg_ref[hl, cs, :]], axis=0) for hl in heads]
            res = every(_dot, lhs, states)
            yield
            vn = [uw_ref[hl, cs, :HEAD_DIM] - res[hl][:c_rows] for hl in heads]
            upd = [_dot_tn(kd_ref[hl, cs, :], vn[hl]) for hl in heads]
            yield
            for hl in heads:
                vn_ref[hl, cs, :] = vn[hl]
                oi_ref[hl, cs, :] = res[hl][c_rows:]
                states[hl] = states[hl] * el_ref[hl, ci * SUBLANES:ci * SUBLANES + 1, :] + upd[hl]

    def finish(gi):
        rs = slice(gi * g_rows, (gi + 1) * g_rows)
        intra = [_dot(attn_ref[hl, rs, :], vn_ref[hl, rs, :]) for hl in heads]
        yield
        for hl, hs in zip(heads, head_lanes):
            o = oi_ref[hl, rs, :] + intra[hl]
            o = o * lax.rsqrt(jnp.mean(jnp.square(o), axis=-1, keepdims=True) + RMS_EPS) * ng_ref[...]
            z = zs_ref[rs, hs]
            op_ref[rs, hs] = (o * (z * jax.nn.sigmoid(z))).astype(op_ref.dtype)

    stage_counts = (5, 1 + 13 * (GDN_HEADS // _GDN_HEAD_BATCH), 2 * cpg, 1)

    def interleave(pipes):
        done = [0] * len(pipes)
        live = set(range(len(pipes)))
        while live:
            i = min(live, key=lambda p: (done[p] + 1) / (pipes[p][1] + 1))
            try:
                next(pipes[i][0])
                done[i] += 1
            except StopIteration:
                live.remove(i)

    stages = (project, prepare, recur, finish)
    for rnd in range(n_groups + len(stages) - 1):
        pipes = [(stage(rnd - lag), stage_counts[lag]) for lag, stage in enumerate(stages)
                 if 0 <= rnd - lag < n_groups]
        interleave(pipes)

    for idx in range(3):
        for w in range(n_wrap):
            tail_ref[idx, w] = tails[idx][w]
    for hl in heads:
        s_ref[hl] = states[hl]
    o_ref[...] = jnp.dot(perm, op_ref[...], preferred_element_type=F32).astype(o_ref.dtype)


def _gdn_mixer(x, w_in_all, layer, conv_w, alog_row, dtb_row, norm_g, *, bsz, length):
    tt = min(_GDN_TT, length)
    nt = length // tt
    full = lambda a: pl.BlockSpec(a.shape, lambda b, t: (0,) * a.ndim)
    per_head = lambda rows, cols: pltpu.VMEM((GDN_HEADS, rows, cols), F32)
    tile = lambda: pltpu.VMEM((tt, GDN_DIM), F32)
    return pl.pallas_call(
        _gdn_kernel,
        out_shape=jax.ShapeDtypeStruct((bsz * length, GDN_DIM), _MXU_DTYPE),
        grid=(bsz, nt),
        in_specs=[pl.BlockSpec((tt, D_MODEL), lambda b, t: (b * nt + t, 0)),
                  pl.BlockSpec((None,) + w_in_all.shape[1:], lambda b, t: (layer, 0, 0)),
                  full(conv_w), full(alog_row), full(dtb_row), full(norm_g)],
        out_specs=pl.BlockSpec((tt, GDN_DIM), lambda b, t: (b * nt + t, 0)),
        scratch_shapes=[pltpu.VMEM((3, GDN_CONV - 1, SUBLANES, GDN_DIM), F32),
                        tile(), tile(), tile(), tile(),
                        per_head(tt, 2 * HEAD_DIM), per_head(tt, HEAD_DIM), per_head(tt, HEAD_DIM),
                        per_head(tt, _GDN_G), per_head((tt // GDN_CHUNK) * SUBLANES, HEAD_DIM),
                        per_head(tt, HEAD_DIM), per_head(tt, HEAD_DIM),
                        pltpu.VMEM((tt, GDN_DIM), _MXU_DTYPE),
                        per_head(HEAD_DIM, HEAD_DIM)],
        compiler_params=pltpu.CompilerParams(
            dimension_semantics=("parallel", "arbitrary"), vmem_limit_bytes=_VMEM_LIMIT),
    )(x, w_in_all, conv_w, alog_row, dtb_row, norm_g)


def _s5_discretize_kernel(are_ref, aim_ref, ldt_ref, bre_ref, bim_ref,
                          abre_ref, abim_ref, bbre_ref, bbim_ref):
    a_re = are_ref[...]
    a_im = aim_ref[...]
    dt = jnp.exp(ldt_ref[...])
    mag = jnp.exp(a_re * dt)
    abar_re = mag * jnp.cos(a_im * dt)
    abar_im = mag * jnp.sin(a_im * dt)
    den = jnp.square(a_re) + jnp.square(a_im)
    n_re = abar_re - 1.0
    n_im = abar_im
    f_re = (n_re * a_re + n_im * a_im) / den
    f_im = (n_im * a_re - n_re * a_im) / den
    abre_ref[...] = abar_re
    abim_ref[...] = abar_im
    bbre_ref[...] = f_re * bre_ref[...] - f_im * bim_ref[...]
    bbim_ref[...] = f_re * bim_ref[...] + f_im * bre_ref[...]


def _s5_discretize(a_re, a_im, log_dt, b_re, b_im):
    rows = S5_GROUPS * S5_GROUP
    rep = lambda t: jnp.repeat(t, S5_GROUP, axis=0)
    b2 = lambda t: jnp.transpose(t, (0, 2, 1)).reshape(rows, S5_STATE)
    ldt = jnp.broadcast_to(log_dt[:, None], (S5_GROUPS, S5_STATE))
    shp = jax.ShapeDtypeStruct((rows, S5_STATE), F32)
    return pl.pallas_call(_s5_discretize_kernel, out_shape=(shp, shp, shp, shp))(
        rep(a_re), rep(a_im), rep(ldt), b2(b_re), b2(b_im))


def _s5_kernel(x_ref, wu_ref, wb_ref, wc_ref, are_ref, aim_ref, d_ref, wg_ref, bg_ref, o_ref,
               us_ref, hs_ref, hre_ref, him_ref):
    nb, tt, _ = x_ref.shape
    pitch = _S5_PITCH
    n_lt = S5_DIM // LANES
    half = n_lt // 2

    @pl.when(pl.program_id(0) == 0)
    def _():
        hre_ref[...] = jnp.zeros_like(hre_ref)
        him_ref[...] = jnp.zeros_like(him_ref)

    us_ref[...] = jnp.dot(_mxu(x_ref[...].reshape(nb * tt, D_MODEL)), wu_ref[...],
                          preferred_element_type=F32)
    n_cp = _S5_CB // 2

    def drive(cp):
        for c in (2 * cp, 2 * cp + 1):
            bu = jnp.dot(_mxu(us_ref[:, c * LANES:(c + 1) * LANES]), wb_ref[c], preferred_element_type=F32)
            for b in range(nb):
                s = (c % 2) * nb + b
                for lt in range(n_lt):
                    hs_ref[cp * n_lt + lt, s * pitch:s * pitch + tt, :] = (
                        bu[b * tt:(b + 1) * tt, lt * LANES:(lt + 1) * LANES])

    def scan(cps):
        tiles = [(cp, l) for cp in cps for l in range(half)]
        a_re = [are_ref[cp * half + l] for cp, l in tiles]
        a_im = [aim_ref[cp * half + l] for cp, l in tiles]

        def step(t, carry):
            out = []
            rows = pl.ds(t, SUBLANES, stride=pitch)
            for i, (cp, l) in enumerate(tiles):
                h_re, h_im = carry[2 * i], carry[2 * i + 1]
                sl_re = cp * n_lt + l
                sl_im = cp * n_lt + half + l
                n_re = a_re[i] * h_re - a_im[i] * h_im + hs_ref[sl_re, rows, :]
                n_im = a_re[i] * h_im + a_im[i] * h_re + hs_ref[sl_im, rows, :]
                hs_ref[sl_re, rows, :] = n_re
                hs_ref[sl_im, rows, :] = n_im
                out += [n_re, n_im]
            return tuple(out)

        init = []
        for cp, l in tiles:
            init += [hre_ref[cp * half + l], him_ref[cp * half + l]]
        fin = lax.fori_loop(0, tt, step, tuple(init), unroll=_S5_UNROLL)
        for i, (cp, l) in enumerate(tiles):
            hre_ref[cp * half + l] = fin[2 * i]
            him_ref[cp * half + l] = fin[2 * i + 1]

    def readout(cp):
        out = {}
        for c in (2 * cp, 2 * cp + 1):
            for b in range(nb):
                s = (c % 2) * nb + b
                hcat = jnp.concatenate(
                    [hs_ref[cp * n_lt + lt, s * pitch:s * pitch + tt, :] for lt in range(n_lt)], axis=1)
                out[(b, c)] = jnp.dot(_mxu(hcat), wc_ref[c], preferred_element_type=F32)
        return out

    ys = {}
    for cp in range(n_cp):
        drive(cp)
    for c0 in range(0, n_cp, _S5_SCAN_PAIRS):
        scan(range(c0, c0 + _S5_SCAN_PAIRS))
    for cp in range(n_cp):
        ys.update(readout(cp))

    for b in range(nb):
        y = jnp.concatenate([ys[(b, c)] for c in range(_S5_CB)], axis=1)
        y = y + d_ref[...] * us_ref[b * tt:(b + 1) * tt, :]
        zg = jax.nn.gelu(y)
        gate = jnp.dot(_mxu(zg), wg_ref[...], preferred_element_type=F32) + bg_ref[...]
        o_ref[b] = (zg * jax.nn.sigmoid(gate)).astype(o_ref.dtype)


def _s5_mixer(x, w_u, wb, wc, a_tab_re, a_tab_im, d_skip, w_glu, b_glu):
    bsz, length, _ = x.shape
    tt = _S5_TT
    n_slab = (_S5_CB // 2) * (S5_DIM // LANES)
    n_pair = n_slab // 2
    full = lambda shape: pl.BlockSpec(shape, lambda t: (0,) * len(shape))
    return pl.pallas_call(
        _s5_kernel,
        out_shape=jax.ShapeDtypeStruct((bsz, length, S5_DIM), _MXU_DTYPE),
        grid=(length // tt,),
        in_specs=[pl.BlockSpec((bsz, tt, D_MODEL), lambda t: (0, t, 0)),
                  full(w_u.shape), full(wb.shape), full(wc.shape), full(a_tab_re.shape), full(a_tab_im.shape),
                  full(d_skip.shape), full(w_glu.shape), full(b_glu.shape)],
        out_specs=pl.BlockSpec((bsz, tt, S5_DIM), lambda t: (0, t, 0)),
        scratch_shapes=[pltpu.VMEM((bsz * tt, S5_DIM), F32),
                        pltpu.VMEM((n_slab, SUBLANES * _S5_PITCH, LANES), F32),
                        pltpu.VMEM((n_pair, SUBLANES, LANES), F32),
                        pltpu.VMEM((n_pair, SUBLANES, LANES), F32)],
        compiler_params=pltpu.CompilerParams(
            dimension_semantics=("arbitrary",), vmem_limit_bytes=_VMEM_LIMIT),
    )(x, w_u, wb, wc, a_tab_re, a_tab_im, d_skip, w_glu, b_glu)


def _s5_tables(abar_re, abar_im, bbar_re, bbar_im, c_re, c_im, bsz):
    gpb = S5_GROUPS // _S5_CB
    eye = jnp.eye(gpb, dtype=F32)

    def drive(bb):
        t = bb.reshape(_S5_CB, gpb, S5_GROUP, S5_STATE)
        t = t[:, :, :, None, :] * eye[None, :, None, :, None]
        return t.reshape(_S5_CB, gpb * S5_GROUP, gpb * S5_STATE)

    def readout(cc):
        t = cc.reshape(_S5_CB, gpb, S5_GROUP, S5_STATE)
        t = jnp.transpose(t, (0, 3, 1, 2))[:, None, :, :, :] * eye[None, :, None, :, None]
        return t.reshape(_S5_CB, gpb * S5_STATE, gpb * S5_GROUP)

    wb = jnp.concatenate([drive(bbar_re), drive(bbar_im)], axis=2).astype(_MXU_DTYPE)
    wc = jnp.concatenate([readout(c_re), -readout(c_im)], axis=1).astype(_MXU_DTYPE)

    def a_tab(ab):
        flat = ab[::S5_GROUP].reshape(_S5_CB // 2, 2, 1, (gpb * S5_STATE) // LANES, LANES)
        t = jnp.broadcast_to(flat, (_S5_CB // 2, 2, bsz, (gpb * S5_STATE) // LANES, LANES))
        t = jnp.transpose(t, (0, 3, 1, 2, 4))
        return t.reshape((_S5_CB // 2) * ((gpb * S5_STATE) // LANES), 2 * bsz, LANES)

    return wb, wc, a_tab(abar_re), a_tab(abar_im)


def _mixout_kernel(mix_ref, x_ref, wq_ref, k_ref, v_ref, wo_ref, g_ref, b_ref, o_ref):
    tm = x_ref.shape[0]
    sub = tm // _MIX_SPLIT
    parts = [slice(s * sub, (s + 1) * sub) for s in range(_MIX_SPLIT)]
    head_lanes = [slice(hh * HEAD_DIM, (hh + 1) * HEAD_DIM) for hh in range(XA_HEADS)]
    mix_dim = mix_ref.shape[1]
    xq = [jnp.dot(_mxu(x_ref[rs, :]), wq_ref[...], preferred_element_type=F32) for rs in parts]
    scores = [[_dot_nt(q[:, sl], k_ref[:, sl]) * (HEAD_DIM ** -0.5) for sl in head_lanes] for q in xq]
    h_mix = [jnp.dot(_mxu(mix_ref[rs, :]), wo_ref[0:mix_dim, :], preferred_element_type=F32) for rs in parts]
    cross = []
    for part_scores in scores:
        outs = []
        for s, sl in zip(part_scores, head_lanes):
            e = jnp.exp(s - jnp.max(s, axis=-1, keepdims=True))
            p = e / jnp.sum(e, axis=-1, keepdims=True)
            outs.append(_dot(p, v_ref[:, sl]))
        cross.append(jnp.concatenate(outs, axis=1))
    for rs, hm, cr in zip(parts, h_mix, cross):
        h = hm + jnp.dot(_mxu(cr), wo_ref[mix_dim:, :], preferred_element_type=F32)
        o_ref[rs, :] = _layer_norm(DN_ALPHA * x_ref[rs, :] + h, g_ref[...], b_ref[...])


def _tail_kernel(mix_ref, x_ref, wq_ref, k_ref, v_ref, wo_ref, g1_ref, b1_ref, w1_ref, w2_ref, g2_ref,
                 b2_ref, o_ref, x1_ref, h_ref):
    _mixout_kernel(mix_ref, x_ref, wq_ref, k_ref, v_ref, wo_ref, g1_ref, b1_ref, x1_ref)
    _mlp_kernel(x1_ref, w1_ref, w2_ref, g2_ref, b2_ref, o_ref, h_ref)


def _layer_tail(mix, x, w_xq, kv, w_o, ln1_g, ln1_b, w1, w2, ln2_g, ln2_b, *, layer, bsz, length):
    tm = min(_MIX_TM, length)
    nt = length // tm
    mem_len = kv.shape[0] // bsz
    row = lambda width: pl.BlockSpec((tm, width), lambda b, t: (b * nt + t, 0))
    full = lambda a: pl.BlockSpec(a.shape, lambda b, t: (0,) * a.ndim, pipeline_mode=pl.Buffered(1))
    of_layer = lambda a: pl.BlockSpec((None,) + a.shape[1:], lambda b, t: (layer, 0, 0),
                                     pipeline_mode=pl.Buffered(1))
    return pl.pallas_call(
        _tail_kernel,
        out_shape=jax.ShapeDtypeStruct(x.shape, F32),
        grid=(bsz, nt),
        in_specs=[row(mix.shape[1]), row(D_MODEL), full(w_xq),
                  pl.BlockSpec((mem_len, XA_DIM), lambda b, t: (b, 0)),
                  pl.BlockSpec((mem_len, XA_DIM), lambda b, t: (b, 1)),
                  of_layer(w_o), full(ln1_g), full(ln1_b), of_layer(w1), of_layer(w2), full(ln2_g), full(ln2_b)],
        out_specs=row(D_MODEL),
        scratch_shapes=[pltpu.VMEM((tm, D_MODEL), F32), pltpu.VMEM((tm, D_FF), _MXU_DTYPE)],
        compiler_params=pltpu.CompilerParams(
            dimension_semantics=("parallel", "parallel"), vmem_limit_bytes=_VMEM_LIMIT),
    )(mix, x, w_xq, kv, kv, w_o, ln1_g, ln1_b, w1, w2, ln2_g, ln2_b)


def _mlp_kernel(x_ref, w1_ref, w2_ref, g_ref, b_ref, o_ref, h_ref):
    tm = x_ref.shape[0]
    sub = tm // _MLP_SPLIT
    for s in range(_MLP_SPLIT):
        rs = slice(s * sub, (s + 1) * sub)
        xb = _mxu(x_ref[rs, :])
        for j in range(D_FF // _MLP_TF):
            fs = slice(j * _MLP_TF, (j + 1) * _MLP_TF)
            hid = jnp.dot(xb, w1_ref[:, fs], preferred_element_type=F32)
            h_ref[rs, fs] = _mxu(jnp.square(jnp.maximum(hid, 0.0)))
        y = jnp.dot(h_ref[rs, :], w2_ref[...], preferred_element_type=F32)
        o_ref[rs, :] = _layer_norm(DN_ALPHA * x_ref[rs, :] + y, g_ref[...], b_ref[...])


def _row(v, offset=0):
    width = -(-(offset + v.shape[0]) // LANES) * LANES
    return jnp.zeros((1, width), F32).at[0, offset:offset + v.shape[0]].set(v.astype(F32))


def kernel(x, mem, w_kv_mem, w_o, ln1_g, ln1_b, ln2_g, ln2_b, mlp_w1, mlp_w2, gdn_w_in, gdn_conv_w,
           gdn_a_log, gdn_dt_bias, gdn_norm_g, s5_w_in, s5_a_re, s5_a_im, s5_b_re, s5_b_im, s5_c_re,
           s5_c_im, s5_log_dt, s5_d, s5_w_glu, s5_b_glu):
    bsz, length, _ = x.shape
    tokens = bsz * length
    xf = x.reshape(tokens, D_MODEL)
    memf = mem.reshape(bsz * mem.shape[1], D_MODEL)
    qkvz = 4 * GDN_DIM
    gdn_w_all = gdn_w_in[:, :, :qkvz + LANES].astype(_MXU_DTYPE)
    w1_all = mlp_w1.astype(_MXU_DTYPE)
    w2_all = mlp_w2.astype(_MXU_DTYPE)
    wo_all = w_o.astype(_MXU_DTYPE)
    for i in range(DEPTH):
        j = i // 2
        kv = _matmul(memf, w_kv_mem[i].astype(_MXU_DTYPE), tm=memf.shape[0], tn=2 * XA_DIM)
        if i % 2 == 0:
            w_xq = gdn_w_in[j][:, qkvz + 2 * GDN_HEADS:]
            mix = _gdn_mixer(xf, gdn_w_all, j, gdn_conv_w[j].astype(F32), _row(gdn_a_log[j], GDN_HEADS),
                             _row(gdn_dt_bias[j], GDN_HEADS), _row(gdn_norm_g[j]), bsz=bsz, length=length)
        else:
            w_xq = s5_w_in[j][:, S5_DIM:]
            abar_re, abar_im, bbar_re, bbar_im = _s5_discretize(
                s5_a_re[j].astype(F32), s5_a_im[j].astype(F32), s5_log_dt[j].astype(F32),
                s5_b_re[j].astype(F32), s5_b_im[j].astype(F32))
            wb, wc, a_tab_re, a_tab_im = _s5_tables(
                abar_re, abar_im, bbar_re, bbar_im, s5_c_re[j].astype(F32), s5_c_im[j].astype(F32), bsz)
            mix = _s5_mixer(xf.reshape(bsz, length, D_MODEL), s5_w_in[j][:, :S5_DIM].astype(_MXU_DTYPE),
                            wb, wc, a_tab_re, a_tab_im, _row(s5_d[j]), s5_w_glu[j].astype(_MXU_DTYPE),
                            _row(s5_b_glu[j]))
            mix = mix.reshape(tokens, S5_DIM)
        xf = _layer_tail(mix, xf, w_xq.astype(_MXU_DTYPE), kv, wo_all, _row(ln1_g[i]), _row(ln1_b[i]),
                         w1_all, w2_all, _row(ln2_g[i]), _row(ln2_b[i]), layer=i, bsz=bsz, length=length)
    return xf.reshape(bsz, length, D_MODEL)
```
